```python
import math
import jax
import jax.numpy as jnp
from jax import lax
import numpy as np

D_MODEL = 1024
BATCH = 8
SEQ = 4096
DEPTH = 4

GRID_W = 64
CTX_LEN = 256
D_MIX = D_MODEL
D_S5 = D_MIX // 2
S5_GROUP = 16
S5_GROUPS = D_S5 // S5_GROUP
S5_STATE = 64
D_LRU = D_MIX - D_S5
LRU_HEADS = 8
LRU_HEAD_DIM = D_LRU // LRU_HEADS
CONV_W = 4
CONV_LEFT = CONV_W // 2
LRU_C = 8.0
D_IN = D_S5 + 2 * D_LRU
D_FF = -(-(8 * D_MODEL) // (3 * 256)) * 256
N_DIR = 2
EPS = 1e-6

kernel_name = "hybrid_s5_rglru_dit_block"


def rms_norm(x, g):
    xf = x.astype(jnp.float32)
    var = jnp.mean(xf * xf, axis=-1, keepdims=True)
    return (xf * lax.rsqrt(var + EPS)).astype(x.dtype) * g


def adaln(cond, w, b):
    m = (jax.nn.silu(cond) @ w + b)[..., None, :]
    return jnp.split(m, 6, axis=-1)


def to_col_major(t, rows):
    b, l, ch = t.shape
    return t.reshape(b, rows, GRID_W, ch).transpose(0, 2, 1, 3).reshape(b, l, ch)


def from_col_major(t, rows):
    b, l, ch = t.shape
    return t.reshape(b, GRID_W, rows, ch).transpose(0, 2, 1, 3).reshape(b, l, ch)


def _real_combine(left, right):
    a1, b1 = left
    a2, b2 = right
    return a1 * a2, a2 * b1 + b2


def _cplx_combine(left, right):
    ar1, ai1, br1, bi1 = left
    ar2, ai2, br2, bi2 = right
    return (ar2 * ar1 - ai2 * ai1,
            ar2 * ai1 + ai2 * ar1,
            ar2 * br1 - ai2 * bi1 + br2,
            ar2 * bi1 + ai2 * br1 + bi2)


def linear_scan(a, b, h0, reverse):
    if h0 is not None:
        edge = -1 if reverse else 0
        b = b.at[:, edge].add(a[:, edge] * h0)
    _, h = lax.associative_scan(_real_combine, (a, b), axis=1, reverse=reverse)
    return h


def s5_discretise(a_re, a_im, log_dt, b_re, b_im):
    dt = jnp.exp(log_dt)[:, None]
    mag = jnp.exp(a_re * dt)
    ab_re = mag * jnp.cos(a_im * dt)
    ab_im = mag * jnp.sin(a_im * dt)
    den = a_re * a_re + a_im * a_im
    nr = ab_re - 1.0
    f_re = (nr * a_re + ab_im * a_im) / den
    f_im = (ab_im * a_re - nr * a_im) / den
    bb_re = f_re[..., None] * b_re - f_im[..., None] * b_im
    bb_im = f_re[..., None] * b_im + f_im[..., None] * b_re
    return ab_re, ab_im, bb_re, bb_im


def s5_scan(u, disc, s0, reverse):
    ab_re, ab_im, bb_re, bb_im = disc
    b, l, _ = u.shape
    ug = u.reshape(b, l, S5_GROUPS, S5_GROUP)
    x_re = jnp.einsum('blgp,gnp->blgn', ug, bb_re)
    x_im = jnp.einsum('blgp,gnp->blgn', ug, bb_im)
    if s0 is not None:
        edge = -1 if reverse else 0
        s0_re, s0_im = s0
        x_re = x_re.at[:, edge].add(ab_re * s0_re - ab_im * s0_im)
        x_im = x_im.at[:, edge].add(ab_re * s0_im + ab_im * s0_re)
    a_re = jnp.broadcast_to(ab_re, (1, l) + ab_re.shape)
    a_im = jnp.broadcast_to(ab_im, (1, l) + ab_im.shape)
    _, _, s_re, s_im = lax.associative_scan(_cplx_combine, (a_re, a_im, x_re, x_im), axis=1, reverse=reverse)
    return s_re, s_im


def s5_readout(s_re, s_im, c_re, c_im):
    y = jnp.einsum('blgn,gpn->blgp', s_re, c_re) - jnp.einsum('blgn,gpn->blgp', s_im, c_im)
    b, l = y.shape[:2]
    return y.reshape(b, l, D_S5)


def s5_glu(y, w_glu, b_glu):
    y = jax.nn.gelu(y)
    return y * jax.nn.sigmoid(y @ w_glu + b_glu)


def centred_depthwise_conv(x, w, b):
    l = x.shape[1]
    xp = jnp.pad(x, ((0, 0), (CONV_LEFT, CONV_W - 1 - CONV_LEFT), (0, 0)))
    out = xp[:, 0:l] * w[0]
    for k in range(1, CONV_W):
        out = out + xp[:, k:k + l] * w[k]
    return out + b


def block_diag(x, w, b):
    xh = x.reshape(x.shape[:-1] + (LRU_HEADS, LRU_HEAD_DIM))
    return jnp.einsum('blhi,hij->blhj', xh, w).reshape(x.shape) + b


def rglru_coeffs(x, w_rg, b_rg, w_ig, b_ig, lam):
    r = jax.nn.sigmoid(block_diag(x, w_rg, b_rg))
    i = jax.nn.sigmoid(block_diag(x, w_ig, b_ig))
    log_a = -LRU_C * r * jax.nn.softplus(-lam)
    a = jnp.exp(log_a)
    mult = jnp.sqrt(-jnp.expm1(2.0 * log_a))
    return a, mult * (i * x)


def hybrid_mixer(h_lat, h_ctx, rows, need_ctx, w_in,
                 s5_a_re, s5_a_im, s5_log_dt, s5_b_re, s5_b_im, s5_c_re, s5_c_im, s5_d, s5_w_glu, s5_b_glu,
                 lru_conv_w, lru_conv_b, lru_w_rg, lru_b_rg, lru_w_ig, lru_b_ig, lru_lambda, w_out):
    split = [D_S5, D_S5 + D_LRU]
    u_lat, xr_lat, gr_lat = jnp.split(h_lat @ w_in, split, axis=-1)
    u_ctx, xr_ctx, gr_ctx = jnp.split(h_ctx @ w_in, split, axis=-1)

    ys_lat, ys_ctx = [s5_d * u_lat], [s5_d * u_ctx]
    for d, reverse in enumerate((False, True)):
        disc = s5_discretise(s5_a_re[d], s5_a_im[d], s5_log_dt[d], s5_b_re[d], s5_b_im[d])
        edge = 0 if reverse else -1
        sc_re, sc_im = s5_scan(u_ctx, disc, None, reverse)
        sl_re, sl_im = s5_scan(u_lat, disc, (sc_re[:, edge], sc_im[:, edge]), reverse)
        ys_lat.append(s5_readout(sl_re, sl_im, s5_c_re[d], s5_c_im[d]))
        if need_ctx:
            ys_ctx.append(s5_readout(sc_re, sc_im, s5_c_re[d], s5_c_im[d]))
    y_s5_lat = s5_glu(ys_lat[0] + ys_lat[1] + ys_lat[2], s5_w_glu, s5_b_glu)

    xc_lat = centred_depthwise_conv(to_col_major(xr_lat, rows), lru_conv_w, lru_conv_b)
    xc_ctx = centred_depthwise_conv(xr_ctx, lru_conv_w, lru_conv_b)
    hs_lat, hs_ctx = [], []
    for d, reverse in enumerate((False, True)):
        edge = 0 if reverse else -1
        a_c, b_c = rglru_coeffs(xc_ctx, lru_w_rg[d], lru_b_rg[d], lru_w_ig[d], lru_b_ig[d], lru_lambda[d])
        h_c = linear_scan(a_c, b_c, None, reverse)
        a_l, b_l = rglru_coeffs(xc_lat, lru_w_rg[d], lru_b_rg[d], lru_w_ig[d], lru_b_ig[d], lru_lambda[d])
        hs_lat.append(linear_scan(a_l, b_l, h_c[:, edge], reverse))
        hs_ctx.append(h_c)
    y_lru_lat = from_col_major(hs_lat[0] + hs_lat[1], rows) * jax.nn.gelu(gr_lat)
    out_lat = jnp.concatenate([y_s5_lat, y_lru_lat], axis=-1) @ w_out

    out_ctx = None
    if need_ctx:
        y_s5_ctx = s5_glu(ys_ctx[0] + ys_ctx[1] + ys_ctx[2], s5_w_glu, s5_b_glu)
        y_lru_ctx = (hs_ctx[0] + hs_ctx[1]) * jax.nn.gelu(gr_ctx)
        out_ctx = jnp.concatenate([y_s5_ctx, y_lru_ctx], axis=-1) @ w_out
    return out_lat, out_ctx


def swiglu(h, w_ffn_in, w_ffn_out):
    gate, up = jnp.split(h @ w_ffn_in, 2, axis=-1)
    return (jax.nn.silu(gate) * up) @ w_ffn_out


def setup_inputs(seed: int = 0) -> dict:
    key = jax.random.key(seed)
    ks = jax.random.split(key, 32)
    f32 = jnp.float32

    def nrm(k, shape, scale):
        return scale * jax.random.normal(k, shape, f32)

    L, G, N, P = DEPTH, S5_GROUPS, S5_STATE, S5_GROUP
    n_idx = jnp.arange(N, dtype=f32)
    a_pow_c = jax.random.uniform(ks[24], (L, N_DIR, D_LRU), f32, 0.9, 0.999)
    a0 = a_pow_c ** (1.0 / LRU_C)
    return {
        "x": nrm(ks[0], (BATCH, SEQ, D_MODEL), 1.0),
        "c": nrm(ks[1], (BATCH, D_MODEL), 1.0),
        "ctx": nrm(ks[2], (BATCH, CTX_LEN, D_MODEL), 1.0),
        "c_ctx": nrm(ks[3], (D_MODEL,), 1.0),
        "w_ada": nrm(ks[4], (L, D_MODEL, 6 * D_MODEL), D_MODEL ** -0.5),
        "b_ada": nrm(ks[5], (L, 6 * D_MODEL), 0.02),
        "norm_gains": 1.0 + nrm(ks[6], (L, 4, D_MODEL), 0.05),
        "w_in": nrm(ks[7], (L, D_MODEL, D_IN), D_MODEL ** -0.5),
        "s5_a_re": -0.5 + nrm(ks[8], (L, N_DIR, G, N), 0.01),
        "s5_a_im": math.pi * n_idx + nrm(ks[9], (L, N_DIR, G, N), 0.01),
        "s5_log_dt": jax.random.uniform(ks[10], (L, N_DIR, G), f32, math.log(1e-3), math.log(1e-1)),
        "s5_b_re": nrm(ks[11], (L, N_DIR, G, N, P), (2 * P) ** -0.5),
        "s5_b_im": nrm(ks[12], (L, N_DIR, G, N, P), (2 * P) ** -0.5),
        "s5_c_re": nrm(ks[13], (L, N_DIR, G, P, N), (2 * N) ** -0.5),
        "s5_c_im": nrm(ks[14], (L, N_DIR, G, P, N), (2 * N) ** -0.5),
        "s5_d": nrm(ks[15], (L, D_S5), 1.0),
        "s5_w_glu": nrm(ks[16], (L, D_S5, D_S5), D_S5 ** -0.5),
        "s5_b_glu": nrm(ks[17], (L, D_S5), 0.01),
        "lru_conv_w": nrm(ks[18], (L, CONV_W, D_LRU), CONV_W ** -0.5),
        "lru_conv_b": nrm(ks[19], (L, D_LRU), 0.01),
        "lru_w_rg": nrm(ks[20], (L, N_DIR, LRU_HEADS, LRU_HEAD_DIM, LRU_HEAD_DIM), LRU_HEAD_DIM ** -0.5),
        "lru_b_rg": nrm(ks[21], (L, N_DIR, D_LRU), 0.01),
        "lru_w_ig": nrm(ks[22], (L, N_DIR, LRU_HEADS, LRU_HEAD_DIM, LRU_HEAD_DIM), LRU_HEAD_DIM ** -0.5),
        "lru_b_ig": nrm(ks[23], (L, N_DIR, D_LRU), 0.01),
        "lru_lambda": jnp.log(a0) - jnp.log1p(-a0),
        "w_out": nrm(ks[25], (L, D_MIX, D_MODEL), D_MIX ** -0.5),
        "w_ffn_in": nrm(ks[26], (L, D_MODEL, 2 * D_FF), D_MODEL ** -0.5),
        "w_ffn_out": nrm(ks[27], (L, D_FF, D_MODEL), D_FF ** -0.5),
    }


def reference(x, c, ctx, c_ctx, w_ada, b_ada, norm_gains, w_in,
              s5_a_re, s5_a_im, s5_log_dt, s5_b_re, s5_b_im, s5_c_re, s5_c_im, s5_d, s5_w_glu, s5_b_glu,
              lru_conv_w, lru_conv_b, lru_w_rg, lru_b_rg, lru_w_ig, lru_b_ig, lru_lambda,
              w_out, w_ffn_in, w_ffn_out):
    rows = x.shape[1] // GRID_W
    for layer in range(DEPTH):
        need_ctx = layer < DEPTH - 1
        g_pre_mix, g_post_mix, g_pre_ffn, g_post_ffn = norm_gains[layer]
        sh1, sc1, gt1, sh2, sc2, gt2 = adaln(c, w_ada[layer], b_ada[layer])
        csh1, csc1, cgt1, csh2, csc2, cgt2 = adaln(c_ctx, w_ada[layer], b_ada[layer])

        h_lat = rms_norm(x, g_pre_mix) * (1.0 + sc1) + sh1
        h_ctx = rms_norm(ctx, g_pre_mix) * (1.0 + csc1) + csh1
        out_lat, out_ctx = hybrid_mixer(
            h_lat, h_ctx, rows, need_ctx, w_in[layer],
            s5_a_re[layer], s5_a_im[layer], s5_log_dt[layer], s5_b_re[layer], s5_b_im[layer],
            s5_c_re[layer], s5_c_im[layer], s5_d[layer], s5_w_glu[layer], s5_b_glu[layer],
            lru_conv_w[layer], lru_conv_b[layer], lru_w_rg[layer], lru_b_rg[layer],
            lru_w_ig[layer], lru_b_ig[layer], lru_lambda[layer], w_out[layer])
        x = x + gt1 * rms_norm(out_lat, g_post_mix)

        f_lat = swiglu(rms_norm(x, g_pre_ffn) * (1.0 + sc2) + sh2, w_ffn_in[layer], w_ffn_out[layer])
        x = x + gt2 * rms_norm(f_lat, g_post_ffn)

        if need_ctx:
            ctx = ctx + cgt1 * rms_norm(out_ctx, g_post_mix)
            f_ctx = swiglu(rms_norm(ctx, g_pre_ffn) * (1.0 + csc2) + csh2, w_ffn_in[layer], w_ffn_out[layer])
            ctx = ctx + cgt2 * rms_norm(f_ctx, g_post_ffn)
    return x
```

```python
import functools
import math

import jax
import jax.numpy as jnp
from jax import lax
from jax.experimental import pallas as pl
from jax.experimental.pallas import tpu as pltpu

F32 = jnp.float32
BF16 = jnp.bfloat16

D_MODEL = 1024
BATCH = 8
SEQ = 4096
DEPTH = 4
GRID_W = 64
GRID_H = SEQ // GRID_W
CTX_LEN = 256
D_S5 = 512
S5_GROUP = 16
S5_GROUPS = 32
S5_STATE = 64
D_LRU = 512
LRU_HEADS = 8
LRU_HEAD_DIM = 64
CONV_W = 4
CONV_LEFT = 2
LRU_C = 8.0
D_IN = D_S5 + 2 * D_LRU
D_FF = 2816
EPS = 1e-6

LANES = 128
SUBLANES = 8
TB = 64
TM = TB * BATCH
N_LAT_BLK = SEQ // TB
N_CTX_BLK = CTX_LEN // TB
N_BLK = N_LAT_BLK + N_CTX_BLK
ROWS_LAT = SEQ * BATCH
ROWS_ALL = (SEQ + CTX_LEN) * BATCH
N_SLAB = D_S5 // LANES
GROUPS_PER_SLAB = LANES // S5_GROUP
SLAB_STATE = GROUPS_PER_SLAB * S5_STATE
LRU_HALF = 256
LRU_RUN = GRID_H
assert TB == LRU_RUN
FF_CHUNKS = ((0, 1024), (1024, 1024), (2048, 768))
VMEM_LIMIT = 56 * 1024 * 1024


def _params(sem):
    return pltpu.CompilerParams(dimension_semantics=sem, vmem_limit_bytes=VMEM_LIMIT)


def _rms(x, g):
    var = jnp.mean(x * x, axis=-1, keepdims=True)
    return x * lax.rsqrt(var + EPS) * g


def _per_batch(x, fn):
    r, ch = x.shape
    return fn(x.reshape(r // BATCH, BATCH, ch)).reshape(r, ch)


def _gelu(x):
    return jax.nn.gelu(x, approximate=True)


def _adaln_kernel(c_ref, w_ref, b_ref, o_ref):
    c = c_ref[...]
    s = (c * jax.nn.sigmoid(c)).astype(BF16)
    o_ref[0] = jnp.dot(s, w_ref[0].astype(BF16), preferred_element_type=F32) + b_ref[0]


def _adaln(cond, w_ada, b_ada):
    rows = cond.shape[0]
    nb = 6
    return pl.pallas_call(
        _adaln_kernel,
        grid=(DEPTH, nb),
        in_specs=[
            pl.BlockSpec((rows, D_MODEL), lambda l, n: (0, 0)),
            pl.BlockSpec((1, D_MODEL, D_MODEL), lambda l, n: (l, 0, n)),
            pl.BlockSpec((1, 1, D_MODEL), lambda l, n: (l, 0, n)),
        ],
        out_specs=pl.BlockSpec((1, rows, D_MODEL), lambda l, n: (l, 0, n)),
        out_shape=jax.ShapeDtypeStruct((DEPTH, rows, 6 * D_MODEL), F32),
        compiler_params=_params(("arbitrary", "arbitrary")),
        name="adaln",
    )(cond, w_ada, b_ada.reshape(DEPTH, 1, 6 * D_MODEL))


def _mod_index(i):
    return (i >= N_LAT_BLK).astype(jnp.int32)


def _premix_kernel(x_ref, mod_ref, g_ref, w_ref, u_ref, xr_ref, gr_ref):
    mod = mod_ref[0]
    shift, scale = mod[:, 0:D_MODEL], mod[:, D_MODEL:2 * D_MODEL]
    h = _rms(x_ref[...], g_ref[...])
    h = _per_batch(h, lambda t: t * (1.0 + scale)[None] + shift[None])
    p = jnp.dot(h.astype(BF16), w_ref[...], preferred_element_type=F32)
    u_ref[...] = p[:, 0:D_S5]
    xr_ref[...] = p[:, D_S5:D_S5 + D_LRU]
    gr_ref[...] = p[:, D_S5 + D_LRU:]


def _premix(x_all, mod, g, w_in):
    row = lambda i: (i, 0)
    fixed = lambda i: (0, 0)
    half = jax.ShapeDtypeStruct((ROWS_ALL, D_S5), F32)
    return pl.pallas_call(
        _premix_kernel,
        grid=(N_BLK,),
        in_specs=[
            pl.BlockSpec((TM, D_MODEL), row),
            pl.BlockSpec((1, BATCH, 6 * D_MODEL), lambda i: (_mod_index(i), 0, 0)),
            pl.BlockSpec((1, D_MODEL), fixed),
            pl.BlockSpec((D_MODEL, D_IN), fixed),
        ],
        out_specs=[pl.BlockSpec((TM, D_S5), row)] * 3,
        out_shape=[half] * 3,
        compiler_params=_params(("parallel",)),
        name="premix",
    )(x_all, mod, g, w_in)


def _s5_kernel(uf_ref, ub_ref, bb_ref, cc_ref, a_ref, yf_ref, yb_ref,
               xf_scr, xb_scr, st_scr):
    @pl.when(pl.program_id(0) == 0)
    def _():
        st_scr[...] = jnp.zeros_like(st_scr)

    n = SLAB_STATE
    for j in range(N_SLAB):
        lanes = slice(j * LANES, (j + 1) * LANES)
        xf_scr[...] = jnp.dot(uf_ref[:, lanes].astype(BF16), bb_ref[0, j],
                              preferred_element_type=F32)
        xb_scr[...] = jnp.dot(ub_ref[:, lanes].astype(BF16), bb_ref[1, j],
                              preferred_element_type=F32)
        bc = lambda v: jnp.broadcast_to(v, (BATCH, n))
        afr, afi = bc(a_ref[0, j, 0:1, :]), bc(a_ref[0, j, 1:2, :])
        abr, abi = bc(a_ref[1, j, 0:1, :]), bc(a_ref[1, j, 1:2, :])

        def body(t, carry):
            sfr, sfi, sbr, sbi = carry
            rf = pl.ds(pl.multiple_of(t * BATCH, BATCH), BATCH)
            rb = pl.ds(pl.multiple_of((TB - 1 - t) * BATCH, BATCH), BATCH)
            nfr = afr * sfr - afi * sfi + xf_scr[rf, 0:n]
            nfi = afr * sfi + afi * sfr + xf_scr[rf, n:2 * n]
            nbr = abr * sbr - abi * sbi + xb_scr[rb, 0:n]
            nbi = abr * sbi + abi * sbr + xb_scr[rb, n:2 * n]
            xf_scr[rf, 0:n] = nfr
            xf_scr[rf, n:2 * n] = nfi
            xb_scr[rb, 0:n] = nbr
            xb_scr[rb, n:2 * n] = nbi
            return nfr, nfi, nbr, nbi

        init = (st_scr[0, j, 0], st_scr[0, j, 1], st_scr[1, j, 0], st_scr[1, j, 1])
        sfr, sfi, sbr, sbi = lax.fori_loop(0, TB, body, init, unroll=2)
        st_scr[0, j, 0] = sfr
        st_scr[0, j, 1] = sfi
        st_scr[1, j, 0] = sbr
        st_scr[1, j, 1] = sbi
        yf_ref[:, lanes] = jnp.dot(xf_scr[...].astype(BF16), cc_ref[0, j],
                                   preferred_element_type=F32)
        yb_ref[:, lanes] = jnp.dot(xb_scr[...].astype(BF16), cc_ref[1, j],
                                   preferred_element_type=F32)


def _s5_fwd_block(i):
    return jnp.where(i < N_CTX_BLK, N_LAT_BLK + i, i - N_CTX_BLK)


def _s5_bwd_block(i):
    return N_BLK - 1 - i


def _s5(u, bb, cc, a):
    out = jax.ShapeDtypeStruct((ROWS_ALL, D_S5), F32)
    w5 = lambda i: (0, 0, 0, 0)
    return pl.pallas_call(
        _s5_kernel,
        grid=(N_BLK,),
        in_specs=[
            pl.BlockSpec((TM, D_S5), lambda i: (_s5_fwd_block(i), 0)),
            pl.BlockSpec((TM, D_S5), lambda i: (_s5_bwd_block(i), 0)),
            pl.BlockSpec((2, N_SLAB, LANES, 2 * SLAB_STATE), w5),
            pl.BlockSpec((2, N_SLAB, 2 * SLAB_STATE, LANES), w5),
            pl.BlockSpec((2, N_SLAB, 2, SLAB_STATE), w5),
        ],
        out_specs=[
            pl.BlockSpec((TM, D_S5), lambda i: (_s5_fwd_block(i), 0)),
            pl.BlockSpec((TM, D_S5), lambda i: (_s5_bwd_block(i), 0)),
        ],
        out_shape=[out, out],
        scratch_shapes=[
            pltpu.VMEM((TM, 2 * SLAB_STATE), F32),
            pltpu.VMEM((TM, 2 * SLAB_STATE), F32),
            pltpu.VMEM((2, N_SLAB, 2, BATCH, SLAB_STATE), F32),
        ],
        compiler_params=_params(("arbitrary",)),
        name="s5_scan",
    )(u, u, bb, cc, a)


def _lru_conv(xe, n, cw_ref, cb_ref):
    out = xe[0:n] * cw_ref[0:1, :][None]
    for k in range(1, CONV_W):
        out = out + xe[k:k + n] * cw_ref[k:k + 1, :][None]
    return out + cb_ref[...][None]


def _lru_coeffs(xc, d, wg_ref, bg_ref, lam_ref, a_scr, b_scr):
    n = xc.shape[0]
    x2 = xc.reshape(n * BATCH, D_LRU)
    for hb in range(D_LRU // LRU_HALF):
        sl = slice(hb * LRU_HALF, (hb + 1) * LRU_HALF)
        xh = x2[:, sl]
        g = jnp.dot(xh.astype(BF16), wg_ref[d, hb], preferred_element_type=F32)
        r = jax.nn.sigmoid(g[:, 0:LRU_HALF] + bg_ref[d, 0:1, sl])
        ig = jax.nn.sigmoid(g[:, LRU_HALF:] + bg_ref[d, 1:2, sl])
        lam = lam_ref[d:d + 1, sl]
        softplus = jnp.maximum(-lam, 0.0) + jnp.log1p(jnp.exp(-jnp.abs(lam)))
        log_a = (-LRU_C * softplus) * r
        a = jnp.exp(log_a)
        b = jnp.sqrt(1.0 - jnp.exp(2.0 * log_a)) * (ig * xh)
        a_scr[d, :, :, sl] = a.reshape(n, BATCH, LRU_HALF)
        b_scr[d, :, :, sl] = b.reshape(n, BATCH, LRU_HALF)


def _lru_scan(n, hf0, hb0, a_scr, b_scr, write_f, write_b):
    def body(r, carry):
        hf, hb = carry
        rb = n - 1 - r
        hf = a_scr[0, r] * hf + b_scr[0, r]
        hb = a_scr[1, rb] * hb + b_scr[1, rb]
        write_f(r, hf)
        write_b(rb, hb)
        return hf, hb
    return lax.fori_loop(0, n, body, (hf0, hb0), unroll=4)


def _lru_kernel(n_runs, n_aliased, xf_ref, xfp_ref, xfn_ref, xb_ref, xbp_ref, xbn_ref,
                cw_ref, cb_ref, wg_ref, bg_ref, lam_ref, st0_ref, *refs):
    hf_ref, hb_ref, st_ref, a_scr, b_scr = refs[n_aliased:]
    i = pl.program_id(0)
    n = LRU_RUN

    @pl.when(i == 0)
    def _():
        st_ref[...] = st0_ref[...]

    def run(x_ref, prev_ref, next_ref, w):
        has_prev = (w > 0).astype(F32)
        has_next = (w < n_runs - 1).astype(F32)
        xe = jnp.concatenate(
            [prev_ref[...] * has_prev, x_ref[...], next_ref[...] * has_next], axis=0)
        return _lru_conv(xe, n, cw_ref, cb_ref)

    _lru_coeffs(run(xf_ref, xfp_ref, xfn_ref, i), 0, wg_ref, bg_ref, lam_ref, a_scr, b_scr)
    _lru_coeffs(run(xb_ref, xbp_ref, xbn_ref, n_runs - 1 - i), 1, wg_ref, bg_ref, lam_ref,
                a_scr, b_scr)

    def write_f(r, h):
        hf_ref[r] = h

    def write_b(r, h):
        hb_ref[r] = h

    hf, hb = _lru_scan(n, st_ref[0], st_ref[1], a_scr, b_scr, write_f, write_b)
    st_ref[0] = hf
    st_ref[1] = hb


def _lru_call(name, n_runs, run_spec, prev_spec, next_spec, x4, weights, st0, hf_in, hb_in):
    full = jax.ShapeDtypeStruct((N_BLK, TB, BATCH, D_LRU), F32)
    st_spec = pl.BlockSpec((2, BATCH, D_LRU), lambda i: (0, 0, 0))
    wspecs = [
        pl.BlockSpec((CONV_W, D_LRU), lambda i: (0, 0)),
        pl.BlockSpec((1, D_LRU), lambda i: (0, 0)),
        pl.BlockSpec((2, 2, LRU_HALF, 2 * LRU_HALF), lambda i: (0, 0, 0, 0)),
        pl.BlockSpec((2, 2, D_LRU), lambda i: (0, 0, 0)),
        pl.BlockSpec((2, D_LRU), lambda i: (0, 0)),
    ]
    fw = lambda i: i
    bw = lambda i: n_runs - 1 - i
    aliased = [] if hf_in is None else [hf_in, hb_in]
    n_in = 6 + len(wspecs) + 1
    return pl.pallas_call(
        functools.partial(_lru_kernel, n_runs, len(aliased)),
        grid=(n_runs,),
        in_specs=[run_spec(fw), prev_spec(fw), next_spec(fw),
                  run_spec(bw), prev_spec(bw), next_spec(bw)] + wspecs
        + [st_spec] + [pl.BlockSpec(memory_space=pl.ANY)] * len(aliased),
        out_specs=[run_spec(fw), run_spec(bw), st_spec],
        out_shape=[full, full, jax.ShapeDtypeStruct((2, BATCH, D_LRU), F32)],
        scratch_shapes=[pltpu.VMEM((2, LRU_RUN, BATCH, D_LRU), F32)] * 2,
        input_output_aliases={n_in + k: k for k in range(len(aliased))},
        compiler_params=_params(("arbitrary",)),
        name=name,
    )(x4, x4, x4, x4, x4, x4, *weights, st0, *aliased)


def _lru(xr, weights):
    x4 = xr.reshape(N_BLK, TB, BATCH, D_LRU)
    st0 = jnp.zeros((2, BATCH, D_LRU), F32)

    c0 = N_LAT_BLK
    clip = lambda k, n: jnp.clip(k, 0, n - 1)
    ctx_run = lambda f: pl.BlockSpec((None, TB, BATCH, D_LRU), lambda i: (c0 + f(i), 0, 0, 0))
    ctx_prev = lambda f: pl.BlockSpec(
        (None, CONV_LEFT, BATCH, D_LRU),
        lambda i: (c0 + clip(f(i) - 1, N_CTX_BLK), TB // CONV_LEFT - 1, 0, 0))
    ctx_next = lambda f: pl.BlockSpec(
        (None, 1, BATCH, D_LRU), lambda i: (c0 + clip(f(i) + 1, N_CTX_BLK), 0, 0, 0))
    hf, hb, st = _lru_call("lru_ctx", N_CTX_BLK, ctx_run, ctx_prev, ctx_next,
                           x4, weights, st0, None, None)

    lat_run = lambda f: pl.BlockSpec((GRID_H, None, BATCH, D_LRU), lambda i: (0, f(i), 0, 0))
    lat_prev = lambda f: pl.BlockSpec(
        (CONV_LEFT, None, BATCH, D_LRU),
        lambda i: (GRID_H // CONV_LEFT - 1, clip(f(i) - 1, GRID_W), 0, 0))
    lat_next = lambda f: pl.BlockSpec(
        (1, None, BATCH, D_LRU), lambda i: (0, clip(f(i) + 1, GRID_W), 0, 0))
    hf, hb, _ = _lru_call("lru_lat", GRID_W, lat_run, lat_prev, lat_next,
                          x4, weights, st, hf, hb)
    return hf.reshape(ROWS_ALL, D_LRU), hb.reshape(ROWS_ALL, D_LRU)


def _mixout_kernel(x_ref, u_ref, yf_ref, yb_ref, gr_ref, hf_ref, hb_ref, mod_ref,
                   d_ref, wglu_ref, bglu_ref, wout_ref, g_ref, o_ref):
    gate = mod_ref[0][:, 2 * D_MODEL:3 * D_MODEL]
    ys = _gelu(d_ref[...] * u_ref[...] + yf_ref[...] + yb_ref[...])
    z = jnp.dot(ys.astype(BF16), wglu_ref[...], preferred_element_type=F32) + bglu_ref[...]
    y_s5 = ys * jax.nn.sigmoid(z)
    y_lru = (hf_ref[...] + hb_ref[...]) * _gelu(gr_ref[...])
    out = jnp.dot(y_s5.astype(BF16), wout_ref[0:D_S5, :], preferred_element_type=F32)
    out = out + jnp.dot(y_lru.astype(BF16), wout_ref[D_S5:, :], preferred_element_type=F32)
    res = _per_batch(_rms(out, g_ref[...]), lambda t: t * gate[None])
    o_ref[...] = x_ref[...] + res


def _mixout(n_blk, x_all, u, yf, yb, gr, hf, hb, mod, d, wglu, bglu, wout, g):
    row = lambda i: (i, 0)
    fixed = lambda i: (0, 0)
    wide = pl.BlockSpec((TM, D_MODEL), row)
    half = pl.BlockSpec((TM, D_S5), row)
    return pl.pallas_call(
        _mixout_kernel,
        grid=(n_blk,),
        in_specs=[wide] + [half] * 6 + [
            pl.BlockSpec((1, BATCH, 6 * D_MODEL), lambda i: (_mod_index(i), 0, 0)),
            pl.BlockSpec((1, D_S5), fixed),
            pl.BlockSpec((D_S5, D_S5), fixed),
            pl.BlockSpec((1, D_S5), fixed),
            pl.BlockSpec((D_MODEL, D_MODEL), fixed),
            pl.BlockSpec((1, D_MODEL), fixed),
        ],
        out_specs=wide,
        out_shape=jax.ShapeDtypeStruct((n_blk * TM, D_MODEL), F32),
        compiler_params=_params(("parallel",)),
        name="mixout",
    )(x_all, u, yf, yb, gr, hf, hb, mod, d, wglu, bglu, wout, g)


def _ffn_kernel(x_ref, mod_ref, gpre_ref, win_ref, wout_ref, gpost_ref, o_ref):
    mod = mod_ref[0]
    shift = mod[:, 3 * D_MODEL:4 * D_MODEL]
    scale = mod[:, 4 * D_MODEL:5 * D_MODEL]
    gate = mod[:, 5 * D_MODEL:6 * D_MODEL]
    x = x_ref[...]
    h = _per_batch(_rms(x, gpre_ref[...]), lambda t: t * (1.0 + scale)[None] + shift[None])
    h = h.astype(BF16)
    f = jnp.zeros((TM, D_MODEL), F32)
    for start, size in FF_CHUNKS:
        gt = jnp.dot(h, win_ref[:, start:start + size], preferred_element_type=F32)
        up = jnp.dot(h, win_ref[:, D_FF + start:D_FF + start + size], preferred_element_type=F32)
        act = (gt * jax.nn.sigmoid(gt) * up).astype(BF16)
        f = f + jnp.dot(act, wout_ref[start:start + size, :], preferred_element_type=F32)
    res = _per_batch(_rms(f, gpost_ref[...]), lambda t: t * gate[None])
    o_ref[...] = x + res


def _ffn(n_blk, x_all, mod, gpre, win, wout, gpost):
    row = lambda i: (i, 0)
    fixed = lambda i: (0, 0)
    once = pl.Buffered(1)
    return pl.pallas_call(
        _ffn_kernel,
        grid=(n_blk,),
        in_specs=[
            pl.BlockSpec((TM, D_MODEL), row),
            pl.BlockSpec((1, BATCH, 6 * D_MODEL), lambda i: (_mod_index(i), 0, 0)),
            pl.BlockSpec((1, D_MODEL), fixed),
            pl.BlockSpec((D_MODEL, 2 * D_FF), fixed, pipeline_mode=once),
            pl.BlockSpec((D_FF, D_MODEL), fixed, pipeline_mode=once),
            pl.BlockSpec((1, D_MODEL), fixed),
        ],
        out_specs=pl.BlockSpec((TM, D_MODEL), row),
        out_shape=jax.ShapeDtypeStruct((n_blk * TM, D_MODEL), F32),
        compiler_params=_params(("parallel",)),
        name="ffn",
    )(x_all, mod, gpre, win, wout, gpost)


def _s5_matrices(a_re, a_im, log_dt, b_re, b_im, c_re, c_im):
    dt = jnp.exp(log_dt)[..., None]
    mag = jnp.exp(a_re * dt)
    ab_re = mag * jnp.cos(a_im * dt)
    ab_im = mag * jnp.sin(a_im * dt)
    den = a_re * a_re + a_im * a_im
    nr = ab_re - 1.0
    f_re = (nr * a_re + ab_im * a_im) / den
    f_im = (ab_im * a_re - nr * a_im) / den
    bb_re = f_re[..., None] * b_re - f_im[..., None] * b_im
    bb_im = f_re[..., None] * b_im + f_im[..., None] * b_re
    eye = jnp.eye(GROUPS_PER_SLAB, dtype=F32)

    def in_map(m):
        m = m.reshape(2, N_SLAB, GROUPS_PER_SLAB, S5_STATE, S5_GROUP)
        return jnp.einsum('djgnp,gh->djgphn', m, eye).reshape(2, N_SLAB, LANES, SLAB_STATE)

    def out_map(m):
        m = m.reshape(2, N_SLAB, GROUPS_PER_SLAB, S5_GROUP, S5_STATE)
        return jnp.einsum('djgpn,gh->djgnhp', m, eye).reshape(2, N_SLAB, SLAB_STATE, LANES)

    bb = jnp.concatenate([in_map(bb_re), in_map(bb_im)], axis=-1).astype(BF16)
    cc = jnp.concatenate([out_map(c_re), out_map(-c_im)], axis=-2).astype(BF16)
    a = jnp.stack([ab_re.reshape(2, N_SLAB, SLAB_STATE), ab_im.reshape(2, N_SLAB, SLAB_STATE)], axis=2)
    return bb, cc, a


def _lru_gate_matrix(w_rg, w_ig):
    heads = LRU_HALF // LRU_HEAD_DIM
    eye = jnp.eye(heads, dtype=F32)

    def bd(w):
        w = w.reshape(2, D_LRU // LRU_HALF, heads, LRU_HEAD_DIM, LRU_HEAD_DIM)
        return jnp.einsum('dbhij,hk->dbhikj', w, eye).reshape(2, D_LRU // LRU_HALF, LRU_HALF, LRU_HALF)

    return jnp.concatenate([bd(w_rg), bd(w_ig)], axis=-1).astype(BF16)


def kernel(x, c, ctx, c_ctx, w_ada, b_ada, norm_gains, w_in, s5_a_re, s5_a_im, s5_log_dt, s5_b_re, s5_b_im, s5_c_re, s5_c_im, s5_d, s5_w_glu, s5_b_glu, lru_conv_w, lru_conv_b, lru_w_rg, lru_b_rg, lru_w_ig, lru_b_ig, lru_lambda, w_out, w_ffn_in, w_ffn_out):
    assert x.shape == (BATCH, SEQ, D_MODEL) and ctx.shape == (BATCH, CTX_LEN, D_MODEL)
    to_rows = lambda t: jnp.transpose(t, (1, 0, 2)).reshape(-1, D_MODEL)
    x_all = jnp.concatenate([to_rows(x), to_rows(ctx)], axis=0)

    cond = jnp.concatenate([c, c_ctx[None], jnp.zeros((2 * BATCH - BATCH - 1, D_MODEL), F32)], axis=0)
    m = _adaln(cond, w_ada, b_ada)
    mod_all = jnp.stack(
        [m[:, :BATCH], jnp.broadcast_to(m[:, BATCH:BATCH + 1], (DEPTH, BATCH, 6 * D_MODEL))], axis=1)

    for l in range(DEPTH):
        need_ctx = l < DEPTH - 1
        n_blk = N_BLK if need_ctx else N_LAT_BLK
        mod = mod_all[l]
        g = norm_gains[l].reshape(4, 1, D_MODEL)
        u, xr, gr = _premix(x_all, mod, g[0], w_in[l].astype(BF16))

        bb, cc, a = _s5_matrices(s5_a_re[l], s5_a_im[l], s5_log_dt[l], s5_b_re[l], s5_b_im[l],
                                 s5_c_re[l], s5_c_im[l])
        yf, yb = _s5(u, bb, cc, a)

        wg = _lru_gate_matrix(lru_w_rg[l], lru_w_ig[l])
        bg = jnp.stack([lru_b_rg[l], lru_b_ig[l]], axis=1)
        hf, hb = _lru(xr, (lru_conv_w[l], lru_conv_b[l][None], wg, bg, lru_lambda[l]))

        x_mid = _mixout(n_blk, x_all, u, yf, yb, gr, hf, hb, mod, s5_d[l][None],
                        s5_w_glu[l].astype(BF16), s5_b_glu[l][None], w_out[l].astype(BF16), g[1])
        x_all = _ffn(n_blk, x_mid, mod, g[2], w_ffn_in[l].astype(BF16), w_ffn_out[l].astype(BF16), g[3])

    return jnp.transpose(x_all.reshape(SEQ, BATCH, D_MODEL), (1, 0, 2))
```

```python
import functools
import math

import jax
import jax.numpy as jnp
from jax import lax
from jax.experimental import pallas as pl
from jax.experimental.pallas import tpu as pltpu

F32 = jnp.float32
BF16 = jnp.bfloat16

D_MODEL = 1024
BATCH = 8
SEQ = 4096
DEPTH = 4
GRID_W = 64
GRID_H = SEQ // GRID_W
CTX_LEN = 256
D_S5 = 512
S5_GROUP = 16
S5_GROUPS = 32
S5_STATE = 64
D_LRU = 512
LRU_HEADS = 8
LRU_HEAD_DIM = 64
CONV_W = 4
CONV_LEFT = 2
LRU_C = 8.0
D_IN = D_S5 + 2 * D_LRU
D_FF = 2816
EPS = 1e-6

LANES = 128
SUBLANES = 8
TB = 64
TM = TB * BATCH
N_LAT_BLK = SEQ // TB
N_CTX_BLK = CTX_LEN // TB
N_BLK = N_LAT_BLK + N_CTX_BLK
ROWS_LAT = SEQ * BATCH
ROWS_ALL = (SEQ + CTX_LEN) * BATCH
N_SLAB = D_S5 // LANES
GROUPS_PER_SLAB = LANES // S5_GROUP
SLAB_STATE = GROUPS_PER_SLAB * S5_STATE
LRU_HALF = 256
LRU_RUN = GRID_H
assert TB == LRU_RUN
FF_CHUNKS = ((0, 1024), (1024, 1024), (2048, 768))
VMEM_LIMIT = 58 * 1024 * 1024


def _params(sem):
    return pltpu.CompilerParams(dimension_semantics=sem, vmem_limit_bytes=VMEM_LIMIT)


def _rms(x, g):
    var = jnp.mean(x * x, axis=-1, keepdims=True)
    return x * lax.rsqrt(var + EPS) * g


def _per_batch(x, fn):
    r, ch = x.shape
    return fn(x.reshape(r // BATCH, BATCH, ch)).reshape(r, ch)


def _gelu(x):
    return jax.nn.gelu(x, approximate=True)


def _sigmoid(x):
    return 0.5 * jnp.tanh(0.5 * x) + 0.5


def _adaln_kernel(c_ref, w_ref, b_ref, o_ref):
    c = c_ref[...]
    s = (c * jax.nn.sigmoid(c)).astype(BF16)
    o_ref[0] = jnp.dot(s, w_ref[0].astype(BF16), preferred_element_type=F32) + b_ref[0]


def _adaln(cond, w_ada, b_ada):
    rows = cond.shape[0]
    nb = 6
    return pl.pallas_call(
        _adaln_kernel,
        grid=(DEPTH, nb),
        in_specs=[
            pl.BlockSpec((rows, D_MODEL), lambda l, n: (0, 0)),
            pl.BlockSpec((1, D_MODEL, D_MODEL), lambda l, n: (l, 0, n)),
            pl.BlockSpec((1, 1, D_MODEL), lambda l, n: (l, 0, n)),
        ],
        out_specs=pl.BlockSpec((1, rows, D_MODEL), lambda l, n: (l, 0, n)),
        out_shape=jax.ShapeDtypeStruct((DEPTH, rows, 6 * D_MODEL), F32),
        compiler_params=_params(("arbitrary", "arbitrary")),
        name="adaln",
    )(cond, w_ada, b_ada.reshape(DEPTH, 1, 6 * D_MODEL))


def _mod_index(i):
    return (i >= N_LAT_BLK).astype(jnp.int32)


def _premix_math(x, mod, g, w_ref, u_ref, xr_ref, gr_ref):
    shift, scale = mod[:, 0:D_MODEL], mod[:, D_MODEL:2 * D_MODEL]
    h = _per_batch(_rms(x, g), lambda t: t * (1.0 + scale)[None] + shift[None])
    p = jnp.dot(h.astype(BF16), w_ref[...], preferred_element_type=F32)
    u_ref[...] = p[:, 0:D_S5].astype(u_ref.dtype)
    xr_ref[...] = p[:, D_S5:D_S5 + D_LRU]
    gr_ref[...] = p[:, D_S5 + D_LRU:].astype(gr_ref.dtype)


def _input_rows(x_ref, ctx_ref):
    blk = jnp.where(pl.program_id(0) < N_LAT_BLK, x_ref[...], ctx_ref[...])
    return jnp.swapaxes(blk, 0, 1).reshape(TM, D_MODEL)


_X_SPEC = pl.BlockSpec((BATCH, TB, D_MODEL), lambda i: (0, jnp.minimum(i, N_LAT_BLK - 1), 0))
_CTX_SPEC = pl.BlockSpec((BATCH, TB, D_MODEL), lambda i: (0, jnp.maximum(i - N_LAT_BLK, 0), 0))


def _premix_kernel(x_ref, ctx_ref, mod_ref, g_ref, w_ref, u_ref, xr_ref, gr_ref):
    _premix_math(_input_rows(x_ref, ctx_ref), mod_ref[0], g_ref[...], w_ref, u_ref, xr_ref, gr_ref)


def _premix(x, ctx, mod, g, w_in):
    row = lambda i: (i, 0)
    fixed = lambda i: (0, 0)
    half = lambda dtype: jax.ShapeDtypeStruct((ROWS_ALL, D_S5), dtype)
    return pl.pallas_call(
        _premix_kernel,
        grid=(N_BLK,),
        in_specs=[
            _X_SPEC,
            _CTX_SPEC,
            pl.BlockSpec((1, BATCH, 6 * D_MODEL), lambda i: (_mod_index(i), 0, 0)),
            pl.BlockSpec((1, D_MODEL), fixed),
            pl.BlockSpec((D_MODEL, D_IN), fixed),
        ],
        out_specs=[pl.BlockSpec((TM, D_S5), row)] * 3,
        out_shape=[half(BF16), half(F32), half(BF16)],
        compiler_params=_params(("parallel",)),
        name="premix",
    )(x, ctx, mod, g, w_in)


def _s5_kernel(uf_ref, ub_ref, bb_ref, cc_ref, a_ref, yf_ref, yb_ref,
               xf_scr, xb_scr, st_scr):
    @pl.when(pl.program_id(0) == 0)
    def _():
        st_scr[...] = jnp.zeros_like(st_scr)

    n = SLAB_STATE
    u_refs = (uf_ref, ub_ref)
    x_scrs = (xf_scr, xb_scr)
    y_refs = (yf_ref, yb_ref)
    for j in range(N_SLAB):
        lanes = slice(j * LANES, (j + 1) * LANES)
        for d in range(2):
            x_scrs[d][j] = jnp.dot(u_refs[d][:, lanes], bb_ref[d, j],
                                   preferred_element_type=F32)
    for j in range(N_SLAB):
        for d in range(2):
            x_scr = x_scrs[d]
            ar, ai = a_ref[d, j, 0], a_ref[d, j, 1]
            sr, si = st_scr[d, j, 0], st_scr[d, j, 1]
            for t in range(TB):
                rows = slice(t * BATCH, (t + 1) * BATCH)
                if d == 1:
                    rows = slice((TB - 1 - t) * BATCH, (TB - t) * BATCH)
                nr = ar * sr - ai * si + x_scr[j, rows, 0:n]
                ni = ar * si + ai * sr + x_scr[j, rows, n:2 * n]
                x_scr[j, rows, 0:n] = nr
                x_scr[j, rows, n:2 * n] = ni
                sr, si = nr, ni
            st_scr[d, j, 0] = sr
            st_scr[d, j, 1] = si
    for j in range(N_SLAB):
        lanes = slice(j * LANES, (j + 1) * LANES)
        for d in range(2):
            y = jnp.dot(x_scrs[d][j].astype(BF16), cc_ref[d, j], preferred_element_type=F32)
            y_refs[d][:, lanes] = y.astype(y_refs[d].dtype)


def _s5_fwd_block(i):
    return jnp.where(i < N_CTX_BLK, N_LAT_BLK + i, i - N_CTX_BLK)


def _s5_bwd_block(i):
    return N_BLK - 1 - i


def _s5(u, bb, cc, a):
    out = jax.ShapeDtypeStruct((ROWS_ALL, D_S5), BF16)
    w5 = lambda i: (0, 0, 0, 0)
    return pl.pallas_call(
        _s5_kernel,
        grid=(N_BLK,),
        in_specs=[
            pl.BlockSpec((TM, D_S5), lambda i: (_s5_fwd_block(i), 0)),
            pl.BlockSpec((TM, D_S5), lambda i: (_s5_bwd_block(i), 0)),
            pl.BlockSpec((2, N_SLAB, LANES, 2 * SLAB_STATE), w5),
            pl.BlockSpec((2, N_SLAB, 2 * SLAB_STATE, LANES), w5),
            pl.BlockSpec((2, N_SLAB, 2, BATCH, SLAB_STATE), lambda i: (0, 0, 0, 0, 0)),
        ],
        out_specs=[
            pl.BlockSpec((TM, D_S5), lambda i: (_s5_fwd_block(i), 0)),
            pl.BlockSpec((TM, D_S5), lambda i: (_s5_bwd_block(i), 0)),
        ],
        out_shape=[out, out],
        scratch_shapes=[
            pltpu.VMEM((N_SLAB, TM, 2 * SLAB_STATE), F32),
            pltpu.VMEM((N_SLAB, TM, 2 * SLAB_STATE), F32),
            pltpu.VMEM((2, N_SLAB, 2, BATCH, SLAB_STATE), F32),
        ],
        compiler_params=_params(("arbitrary",)),
        name="s5_scan",
    )(u, u, bb, cc, a)


def _lru_conv(xe, n, cw_ref, cb_ref):
    out = xe[0:n] * cw_ref[0:1, :][None]
    for k in range(1, CONV_W):
        out = out + xe[k:k + n] * cw_ref[k:k + 1, :][None]
    return out + cb_ref[...][None]


def _lru_coeffs(xc, d, wg_ref, bg_ref, lam_ref, a_scr, b_scr):
    n = xc.shape[0]
    x2 = xc.reshape(n * BATCH, D_LRU)
    for hb in range(D_LRU // LRU_HALF):
        sl = slice(hb * LRU_HALF, (hb + 1) * LRU_HALF)
        xh = x2[:, sl]
        g = jnp.dot(xh.astype(BF16), wg_ref[d, hb], preferred_element_type=F32)
        tr = jnp.tanh(g[:, 0:LRU_HALF] + bg_ref[d, 0:1, sl])
        ti = jnp.tanh(g[:, LRU_HALF:] + bg_ref[d, 1:2, sl])
        lam = lam_ref[d:d + 1, sl]
        softplus = jnp.maximum(-lam, 0.0) + jnp.log1p(jnp.exp(-jnp.abs(lam)))
        k = (-0.5 * LRU_C * math.log2(math.e)) * softplus
        a = jnp.exp2(k * tr + k)
        rest = 1.0 - a * a
        mult = jnp.where(rest > 0.0, rest * lax.rsqrt(rest), 0.0)
        b = mult * (ti * xh + xh)
        a_scr[d, :, :, sl] = a.reshape(n, BATCH, LRU_HALF)
        b_scr[d, :, :, sl] = b.reshape(n, BATCH, LRU_HALF)


def _lru_scan(n, hf0, hb0, a_scr, b_scr, write_f, write_b):
    def body(r, carry):
        hf, hb = carry
        rb = n - 1 - r
        hf = a_scr[0, r] * hf + b_scr[0, r]
        hb = a_scr[1, rb] * hb + b_scr[1, rb]
        write_f(r, hf)
        write_b(rb, hb)
        return hf, hb
    return lax.fori_loop(0, n, body, (hf0, hb0), unroll=4)


def _lru_kernel(n_runs, n_aliased, xf_ref, xfp_ref, xfn_ref, xb_ref, xbp_ref, xbn_ref,
                cw_ref, cb_ref, wg_ref, bg_ref, lam_ref, st0_ref, *refs):
    hf_ref, hb_ref, st_ref, a_scr, b_scr = refs[n_aliased:]
    i = pl.program_id(0)
    n = LRU_RUN

    @pl.when(i == 0)
    def _():
        st_ref[...] = st0_ref[...]

    def run(x_ref, prev_ref, next_ref, w):
        has_prev = (w > 0).astype(F32)
        has_next = (w < n_runs - 1).astype(F32)
        xe = jnp.concatenate(
            [prev_ref[...] * has_prev, x_ref[...], next_ref[...] * has_next], axis=0)
        return _lru_conv(xe, n, cw_ref, cb_ref)

    _lru_coeffs(run(xf_ref, xfp_ref, xfn_ref, i), 0, wg_ref, bg_ref, lam_ref, a_scr, b_scr)
    _lru_coeffs(run(xb_ref, xbp_ref, xbn_ref, n_runs - 1 - i), 1, wg_ref, bg_ref, lam_ref,
                a_scr, b_scr)

    def write_f(r, h):
        hf_ref[r] = h

    def write_b(r, h):
        hb_ref[r] = h

    hf, hb = _lru_scan(n, st_ref[0], st_ref[1], a_scr, b_scr, write_f, write_b)
    st_ref[0] = hf
    st_ref[1] = hb


def _lru_call(name, n_runs, run_spec, prev_spec, next_spec, x4, weights, st0, hf_in, hb_in):
    full = jax.ShapeDtypeStruct((N_BLK, TB, BATCH, D_LRU), F32)
    st_spec = pl.BlockSpec((2, BATCH, D_LRU), lambda i: (0, 0, 0))
    wspecs = [
        pl.BlockSpec((CONV_W, D_LRU), lambda i: (0, 0)),
        pl.BlockSpec((1, D_LRU), lambda i: (0, 0)),
        pl.BlockSpec((2, 2, LRU_HALF, 2 * LRU_HALF), lambda i: (0, 0, 0, 0)),
        pl.BlockSpec((2, 2, D_LRU), lambda i: (0, 0, 0)),
        pl.BlockSpec((2, D_LRU), lambda i: (0, 0)),
    ]
    fw = lambda i: i
    bw = lambda i: n_runs - 1 - i
    aliased = [] if hf_in is None else [hf_in, hb_in]
    n_in = 6 + len(wspecs) + 1
    return pl.pallas_call(
        functools.partial(_lru_kernel, n_runs, len(aliased)),
        grid=(n_runs,),
        in_specs=[run_spec(fw), prev_spec(fw), next_spec(fw),
                  run_spec(bw), prev_spec(bw), next_spec(bw)] + wspecs
        + [st_spec] + [pl.BlockSpec(memory_space=pl.ANY)] * len(aliased),
        out_specs=[run_spec(fw), run_spec(bw), st_spec],
        out_shape=[full, full, jax.ShapeDtypeStruct((2, BATCH, D_LRU), F32)],
        scratch_shapes=[pltpu.VMEM((2, LRU_RUN, BATCH, D_LRU), F32)] * 2,
        input_output_aliases={n_in + k: k for k in range(len(aliased))},
        compiler_params=_params(("arbitrary",)),
        name=name,
    )(x4, x4, x4, x4, x4, x4, *weights, st0, *aliased)


def _lru(xr, weights):
    x4 = xr.reshape(N_BLK, TB, BATCH, D_LRU)
    st0 = jnp.zeros((2, BATCH, D_LRU), F32)

    c0 = N_LAT_BLK
    clip = lambda k, n: jnp.clip(k, 0, n - 1)
    ctx_run = lambda f: pl.BlockSpec((None, TB, BATCH, D_LRU), lambda i: (c0 + f(i), 0, 0, 0))
    ctx_prev = lambda f: pl.BlockSpec(
        (None, CONV_LEFT, BATCH, D_LRU),
        lambda i: (c0 + clip(f(i) - 1, N_CTX_BLK), TB // CONV_LEFT - 1, 0, 0))
    ctx_next = lambda f: pl.BlockSpec(
        (None, 1, BATCH, D_LRU), lambda i: (c0 + clip(f(i) + 1, N_CTX_BLK), 0, 0, 0))
    hf, hb, st = _lru_call("lru_ctx", N_CTX_BLK, ctx_run, ctx_prev, ctx_next,
                           x4, weights, st0, None, None)

    lat_run = lambda f: pl.BlockSpec((GRID_H, None, BATCH, D_LRU), lambda i: (0, f(i), 0, 0))
    lat_prev = lambda f: pl.BlockSpec(
        (CONV_LEFT, None, BATCH, D_LRU),
        lambda i: (GRID_H // CONV_LEFT - 1, clip(f(i) - 1, GRID_W), 0, 0))
    lat_next = lambda f: pl.BlockSpec(
        (1, None, BATCH, D_LRU), lambda i: (0, clip(f(i) + 1, GRID_W), 0, 0))
    hf, hb, _ = _lru_call("lru_lat", GRID_W, lat_run, lat_prev, lat_next,
                          x4, weights, st, hf, hb)
    return hf.reshape(ROWS_ALL, D_LRU), hb.reshape(ROWS_ALL, D_LRU)


def _tail_kernel(first, has_next, *refs):
    if first:
        x_in = _input_rows(refs[0], refs[1])
        refs = refs[2:]
    else:
        x_in = refs[0][...]
        refs = refs[1:]
    (u_ref, yf_ref, yb_ref, gr_ref, hf_ref, hb_ref, mod_ref,
     d_ref, wglu_ref, bglu_ref, wout_ref, gains_ref, wfin_ref, wfout_ref) = refs[:14]
    refs = refs[14:]
    mod = mod_ref[0]
    chunk = lambda k: mod[:, k * D_MODEL:(k + 1) * D_MODEL]
    gate1, shift2, scale2, gate2 = chunk(2), chunk(3), chunk(4), chunk(5)
    f32 = lambda ref: ref[...].astype(F32)

    ys = _gelu(d_ref[...] * f32(u_ref) + f32(yf_ref) + f32(yb_ref))
    z = jnp.dot(ys.astype(BF16), wglu_ref[...], preferred_element_type=F32) + bglu_ref[...]
    y_s5 = ys * _sigmoid(z)
    y_lru = (hf_ref[...] + hb_ref[...]) * _gelu(f32(gr_ref))
    out = jnp.dot(y_s5.astype(BF16), wout_ref[0:D_S5, :], preferred_element_type=F32)
    out = out + jnp.dot(y_lru.astype(BF16), wout_ref[D_S5:, :], preferred_element_type=F32)
    x = x_in + _per_batch(_rms(out, gains_ref[1:2, :]), lambda t: t * gate1[None])

    h = _per_batch(_rms(x, gains_ref[2:3, :]), lambda t: t * (1.0 + scale2)[None] + shift2[None])
    h = h.astype(BF16)
    f = jnp.zeros((TM, D_MODEL), F32)
    for start, size in FF_CHUNKS:
        gt = jnp.dot(h, wfin_ref[:, start:start + size], preferred_element_type=F32)
        up = jnp.dot(h, wfin_ref[:, D_FF + start:D_FF + start + size], preferred_element_type=F32)
        act = (gt * _sigmoid(gt) * up).astype(BF16)
        f = f + jnp.dot(act, wfout_ref[start:start + size, :], preferred_element_type=F32)
    x = x + _per_batch(_rms(f, gains_ref[3:4, :]), lambda t: t * gate2[None])

    if has_next:
        modn_ref, gn_ref, win_ref, o_ref, un_ref, xrn_ref, grn_ref = refs
        o_ref[...] = x
        _premix_math(x, modn_ref[0], gn_ref[...], win_ref, un_ref, xrn_ref, grn_ref)
    else:
        (o_ref,) = refs
        o_ref[...] = jnp.swapaxes(x.reshape(TB, BATCH, D_MODEL), 0, 1)


def _tail(x_src, u, yf, yb, gr, hf, hb, mod, d, wglu, bglu, wout, gains, wfin, wfout,
          mod_next=None, g_next=None, w_in_next=None):
    first = len(x_src) == 2
    has_next = mod_next is not None
    n_blk = N_BLK if has_next else N_LAT_BLK
    rows = n_blk * TM
    row = lambda i: (i, 0)
    fixed = lambda i: (0, 0)
    once = pl.Buffered(1)
    wide = pl.BlockSpec((TM, D_MODEL), row)
    half = pl.BlockSpec((TM, D_S5), row)
    mod_spec = pl.BlockSpec((1, BATCH, 6 * D_MODEL), lambda i: (_mod_index(i), 0, 0))
    in_specs = ([_X_SPEC, _CTX_SPEC] if first else [wide]) + [half] * 6 + [
        mod_spec,
        pl.BlockSpec((1, D_S5), fixed),
        pl.BlockSpec((D_S5, D_S5), fixed, pipeline_mode=once),
        pl.BlockSpec((1, D_S5), fixed),
        pl.BlockSpec((D_MODEL, D_MODEL), fixed, pipeline_mode=once),
        pl.BlockSpec((4, D_MODEL), fixed),
        pl.BlockSpec((D_MODEL, 2 * D_FF), fixed, pipeline_mode=once),
        pl.BlockSpec((D_FF, D_MODEL), fixed, pipeline_mode=once),
    ]
    args = [*x_src, u, yf, yb, gr, hf, hb, mod, d, wglu, bglu, wout, gains, wfin, wfout]
    if has_next:
        in_specs += [mod_spec, pl.BlockSpec((1, D_MODEL), fixed),
                     pl.BlockSpec((D_MODEL, D_IN), fixed, pipeline_mode=once)]
        args += [mod_next, g_next, w_in_next]
        out_specs = [wide] + [half] * 3
        out_shape = [jax.ShapeDtypeStruct((rows, D_MODEL), F32)]
        out_shape += [jax.ShapeDtypeStruct((rows, D_S5), dt) for dt in (BF16, F32, BF16)]
    else:
        out_specs = [pl.BlockSpec((BATCH, TB, D_MODEL), lambda i: (0, i, 0))]
        out_shape = [jax.ShapeDtypeStruct((BATCH, SEQ, D_MODEL), F32)]
    return pl.pallas_call(
        functools.partial(_tail_kernel, first, has_next),
        grid=(n_blk,),
        in_specs=in_specs,
        out_specs=out_specs,
        out_shape=out_shape,
        compiler_params=_params(("parallel",)),
        name="tail",
    )(*args)


def _s5_matrices(a_re, a_im, log_dt, b_re, b_im, c_re, c_im):
    dt = jnp.exp(log_dt)[..., None]
    mag = jnp.exp(a_re * dt)
    ab_re = mag * jnp.cos(a_im * dt)
    ab_im = mag * jnp.sin(a_im * dt)
    den = a_re * a_re + a_im * a_im
    nr = ab_re - 1.0
    f_re = (nr * a_re + ab_im * a_im) / den
    f_im = (ab_im * a_re - nr * a_im) / den
    bb_re = f_re[..., None] * b_re - f_im[..., None] * b_im
    bb_im = f_re[..., None] * b_im + f_im[..., None] * b_re
    eye = jnp.eye(GROUPS_PER_SLAB, dtype=F32)

    def in_map(m):
        m = m.reshape(2, N_SLAB, GROUPS_PER_SLAB, S5_STATE, S5_GROUP)
        return jnp.einsum('djgnp,gh->djgphn', m, eye).reshape(2, N_SLAB, LANES, SLAB_STATE)

    def out_map(m):
        m = m.reshape(2, N_SLAB, GROUPS_PER_SLAB, S5_GROUP, S5_STATE)
        return jnp.einsum('djgpn,gh->djgnhp', m, eye).reshape(2, N_SLAB, SLAB_STATE, LANES)

    bb = jnp.concatenate([in_map(bb_re), in_map(bb_im)], axis=-1).astype(BF16)
    cc = jnp.concatenate([out_map(c_re), out_map(-c_im)], axis=-2).astype(BF16)
    a = jnp.stack([ab_re.reshape(2, N_SLAB, SLAB_STATE), ab_im.reshape(2, N_SLAB, SLAB_STATE)], axis=2)
    a = jnp.broadcast_to(a[:, :, :, None, :], (2, N_SLAB, 2, BATCH, SLAB_STATE))
    return bb, cc, a


def _lru_gate_matrix(w_rg, w_ig):
    heads = LRU_HALF // LRU_HEAD_DIM
    eye = jnp.eye(heads, dtype=F32)

    def bd(w):
        w = w.reshape(2, D_LRU // LRU_HALF, heads, LRU_HEAD_DIM, LRU_HEAD_DIM)
        return jnp.einsum('dbhij,hk->dbhikj', w, eye).reshape(2, D_LRU // LRU_HALF, LRU_HALF, LRU_HALF)

    return jnp.concatenate([bd(w_rg), bd(w_ig)], axis=-1).astype(BF16)


def kernel(x, c, ctx, c_ctx, w_ada, b_ada, norm_gains, w_in, s5_a_re, s5_a_im, s5_log_dt, s5_b_re, s5_b_im, s5_c_re, s5_c_im, s5_d, s5_w_glu, s5_b_glu, lru_conv_w, lru_conv_b, lru_w_rg, lru_b_rg, lru_w_ig, lru_b_ig, lru_lambda, w_out, w_ffn_in, w_ffn_out):
    assert x.shape == (BATCH, SEQ, D_MODEL) and ctx.shape == (BATCH, CTX_LEN, D_MODEL)
    cond = jnp.concatenate([c, c_ctx[None], jnp.zeros((2 * BATCH - BATCH - 1, D_MODEL), F32)], axis=0)
    m = _adaln(cond, w_ada, b_ada)
    mod_all = jnp.stack(
        [m[:, :BATCH], jnp.broadcast_to(m[:, BATCH:BATCH + 1], (DEPTH, BATCH, 6 * D_MODEL))], axis=1)

    u, xr, gr = _premix(x, ctx, mod_all[0], norm_gains[0, 0:1], w_in[0].astype(BF16))
    x_src = (x, ctx)
    for l in range(DEPTH):
        bb, cc, a = _s5_matrices(s5_a_re[l], s5_a_im[l], s5_log_dt[l], s5_b_re[l], s5_b_im[l],
                                 s5_c_re[l], s5_c_im[l])
        yf, yb = _s5(u, bb, cc, a)

        wg = _lru_gate_matrix(lru_w_rg[l], lru_w_ig[l])
        bg = 0.5 * jnp.stack([lru_b_rg[l], lru_b_ig[l]], axis=1)
        hf, hb = _lru(xr, (0.5 * lru_conv_w[l], 0.5 * lru_conv_b[l][None], wg, bg, lru_lambda[l]))

        args = (x_src, u, yf, yb, gr, hf, hb, mod_all[l], s5_d[l][None], s5_w_glu[l].astype(BF16),
                s5_b_glu[l][None], w_out[l].astype(BF16), norm_gains[l],
                w_ffn_in[l].astype(BF16), w_ffn_out[l].astype(BF16))
        if l < DEPTH - 1:
            x_all, u, xr, gr = _tail(*args, mod_all[l + 1], norm_gains[l + 1, 0:1],
                                     w_in[l + 1].astype(BF16))
            x_src = (x_all,)
        else:
            (out,) = _tail(*args)
    return out
```

```python
import functools
import math

import jax
import jax.numpy as jnp
from jax import lax
from jax.experimental import pallas as pl
from jax.experimental.pallas import tpu as pltpu

F32 = jnp.float32
BF16 = jnp.bfloat16

D_MODEL = 1024
BATCH = 8
SEQ = 4096
DEPTH = 4
GRID_W = 64
GRID_H = SEQ // GRID_W
CTX_LEN = 256
D_S5 = 512
S5_GROUP = 16
S5_GROUPS = 32
S5_STATE = 64
D_LRU = 512
LRU_HEADS = 8
LRU_HEAD_DIM = 64
CONV_W = 4
CONV_LEFT = 2
LRU_C = 8.0
D_IN = D_S5 + 2 * D_LRU
D_FF = 2816
EPS = 1e-6

LANES = 128
SUBLANES = 8
TB = 64
TM = TB * BATCH
N_LAT_BLK = SEQ // TB
N_CTX_BLK = CTX_LEN // TB
N_BLK = N_LAT_BLK + N_CTX_BLK
ROWS_LAT = SEQ * BATCH
ROWS_ALL = (SEQ + CTX_LEN) * BATCH
N_SLAB = D_S5 // LANES
GROUPS_PER_SLAB = LANES // S5_GROUP
SLAB_STATE = GROUPS_PER_SLAB * S5_STATE
S5_TILES = SLAB_STATE // LANES
LRU_HALF = 256
LRU_RUN = GRID_H
assert TB == LRU_RUN
FF_CHUNKS = ((0, 1024), (1024, 1024), (2048, 768))
TAIL_SPLIT = 2
TAIL_ROWS = TM // TAIL_SPLIT
VMEM_LIMIT = 58 * 1024 * 1024


def _params(sem):
    return pltpu.CompilerParams(dimension_semantics=sem, vmem_limit_bytes=VMEM_LIMIT)


def _rms(x, g):
    var = jnp.mean(x * x, axis=-1, keepdims=True)
    return x * lax.rsqrt(var + EPS) * g


def _per_batch(x, fn):
    r, ch = x.shape
    return fn(x.reshape(r // BATCH, BATCH, ch)).reshape(r, ch)


def _gelu(x):
    return jax.nn.gelu(x, approximate=True)


def _sigmoid(x):
    return 0.5 * jnp.tanh(0.5 * x) + 0.5


def _adaln_kernel(c_ref, w_ref, b_ref, o_ref):
    c = c_ref[...]
    s = (c * jax.nn.sigmoid(c)).astype(BF16)
    o_ref[0] = jnp.dot(s, w_ref[0].astype(BF16), preferred_element_type=F32) + b_ref[0]


def _adaln(cond, w_ada, b_ada):
    rows = cond.shape[0]
    nb = 6
    return pl.pallas_call(
        _adaln_kernel,
        grid=(DEPTH, nb),
        in_specs=[
            pl.BlockSpec((rows, D_MODEL), lambda l, n: (0, 0)),
            pl.BlockSpec((1, D_MODEL, D_MODEL), lambda l, n: (l, 0, n)),
            pl.BlockSpec((1, 1, D_MODEL), lambda l, n: (l, 0, n)),
        ],
        out_specs=pl.BlockSpec((1, rows, D_MODEL), lambda l, n: (l, 0, n)),
        out_shape=jax.ShapeDtypeStruct((DEPTH, rows, 6 * D_MODEL), F32),
        compiler_params=_params(("arbitrary", "arbitrary")),
        name="adaln",
    )(cond, w_ada, b_ada.reshape(DEPTH, 1, 6 * D_MODEL))


def _mod_index(i):
    return (i >= N_LAT_BLK).astype(jnp.int32)


def _layer_spec(l, shape, **kw):
    zeros = (0,) * len(shape)
    return pl.BlockSpec((None,) + tuple(shape), lambda i: (l,) + zeros, **kw)


def _mod_spec(l):
    return pl.BlockSpec((None, 1, BATCH, 6 * D_MODEL), lambda i: (l, _mod_index(i), 0, 0))


def _premix_math(x, mod, g, w_ref, u_ref, xr_ref, gr_ref):
    shift, scale = mod[:, 0:D_MODEL], mod[:, D_MODEL:2 * D_MODEL]
    h = _per_batch(_rms(x, g), lambda t: t * (1.0 + scale)[None] + shift[None])
    p = jnp.dot(h.astype(BF16), w_ref[...], preferred_element_type=F32)
    u_ref[...] = p[:, 0:D_S5].astype(u_ref.dtype)
    xr_ref[...] = p[:, D_S5:D_S5 + D_LRU]
    gr_ref[...] = p[:, D_S5 + D_LRU:].astype(gr_ref.dtype)


def _input_rows(x_ref, ctx_ref):
    blk = jnp.where(pl.program_id(0) < N_LAT_BLK, x_ref[...], ctx_ref[...])
    return jnp.swapaxes(blk, 0, 1).reshape(TM, D_MODEL)


_X_SPEC = pl.BlockSpec((BATCH, TB, D_MODEL), lambda i: (0, jnp.minimum(i, N_LAT_BLK - 1), 0))
_CTX_SPEC = pl.BlockSpec((BATCH, TB, D_MODEL), lambda i: (0, jnp.maximum(i - N_LAT_BLK, 0), 0))


def _premix_kernel(x_ref, ctx_ref, mod_ref, g_ref, w_ref, u_ref, xr_ref, gr_ref):
    _premix_math(_input_rows(x_ref, ctx_ref), mod_ref[0], g_ref[0:1, :], w_ref, u_ref, xr_ref, gr_ref)


def _premix(l, x, ctx, mod, gains, w_in):
    row = lambda i: (i, 0)
    half = lambda dtype: jax.ShapeDtypeStruct((ROWS_ALL, D_S5), dtype)
    return pl.pallas_call(
        _premix_kernel,
        grid=(N_BLK,),
        in_specs=[
            _X_SPEC,
            _CTX_SPEC,
            _mod_spec(l),
            _layer_spec(l, (4, D_MODEL)),
            _layer_spec(l, (D_MODEL, D_IN)),
        ],
        out_specs=[pl.BlockSpec((TM, D_S5), row)] * 3,
        out_shape=[half(BF16), half(F32), half(BF16)],
        compiler_params=_params(("parallel",)),
        name="premix",
    )(x, ctx, mod, gains, w_in)


def _s5_kernel(uf_ref, ub_ref, bb_ref, cc_ref, a_ref, yf_ref, yb_ref, st_scr):
    @pl.when(pl.program_id(0) == 0)
    def _():
        st_scr[...] = jnp.zeros_like(st_scr)

    u_refs = (uf_ref, ub_ref)
    y_refs = (yf_ref, yb_ref)
    for j in range(N_SLAB):
        lanes = slice(j * LANES, (j + 1) * LANES)
        for d in range(2):
            u = u_refs[d][:, lanes]
            y = None
            for m in range(S5_TILES):
                x = jnp.dot(u, bb_ref[d, j, m], preferred_element_type=F32)
                ar, ai = a_ref[d, j, m, 0], a_ref[d, j, m, 1]
                sr, si = st_scr[d, j, m, 0], st_scr[d, j, m, 1]
                states = [None] * TB
                for t in range(TB):
                    t0 = (TB - 1 - t if d == 1 else t) * BATCH
                    xt = x[t0:t0 + BATCH]
                    sr, si = (ar * sr - ai * si + xt[:, 0:LANES],
                              ar * si + ai * sr + xt[:, LANES:])
                    states[t0 // BATCH] = jnp.concatenate([sr, si], axis=1)
                st_scr[d, j, m, 0] = sr
                st_scr[d, j, m, 1] = si
                part = lax.dot_general(jnp.concatenate(states, axis=0), cc_ref[d, j, m],
                                       (((1,), (0,)), ((), ())), preferred_element_type=F32)
                y = part if y is None else y + part
            y_refs[d][:, lanes] = y.astype(y_refs[d].dtype)


def _s5_fwd_block(i):
    return jnp.where(i < N_CTX_BLK, N_LAT_BLK + i, i - N_CTX_BLK)


def _s5_bwd_block(i):
    return N_BLK - 1 - i


def _s5(l, u, bb, cc, a):
    out = jax.ShapeDtypeStruct((ROWS_ALL, D_S5), BF16)
    return pl.pallas_call(
        _s5_kernel,
        grid=(N_BLK,),
        in_specs=[
            pl.BlockSpec((TM, D_S5), lambda i: (_s5_fwd_block(i), 0)),
            pl.BlockSpec((TM, D_S5), lambda i: (_s5_bwd_block(i), 0)),
            _layer_spec(l, (2, N_SLAB, S5_TILES, LANES, 2 * LANES)),
            _layer_spec(l, (2, N_SLAB, S5_TILES, 2 * LANES, LANES)),
            _layer_spec(l, (2, N_SLAB, S5_TILES, 2, BATCH, LANES)),
        ],
        out_specs=[
            pl.BlockSpec((TM, D_S5), lambda i: (_s5_fwd_block(i), 0)),
            pl.BlockSpec((TM, D_S5), lambda i: (_s5_bwd_block(i), 0)),
        ],
        out_shape=[out, out],
        scratch_shapes=[pltpu.VMEM((2, N_SLAB, S5_TILES, 2, BATCH, LANES), F32)],
        compiler_params=_params(("arbitrary",)),
        name="s5_scan",
    )(u, u, bb, cc, a)


def _lru_conv(xe, n, cw_ref, cb_ref):
    out = xe[0:n] * cw_ref[0:1, :][None]
    for k in range(1, CONV_W):
        out = out + xe[k:k + n] * cw_ref[k:k + 1, :][None]
    return out + cb_ref[...][None]


def _lru_coeffs(xc, d, wg_ref, bg_ref, lam_ref, a_scr, b_scr):
    n = xc.shape[0]
    x2 = xc.reshape(n * BATCH, D_LRU)
    for hb in range(D_LRU // LRU_HALF):
        sl = slice(hb * LRU_HALF, (hb + 1) * LRU_HALF)
        xh = x2[:, sl]
        g = jnp.dot(xh.astype(BF16), wg_ref[d, hb], preferred_element_type=F32)
        tr = jnp.tanh(g[:, 0:LRU_HALF] + bg_ref[d, 0:1, sl])
        ti = jnp.tanh(g[:, LRU_HALF:] + bg_ref[d, 1:2, sl])
        lam = lam_ref[d:d + 1, sl]
        softplus = jnp.maximum(-lam, 0.0) + jnp.log1p(jnp.exp(-jnp.abs(lam)))
        k = (-0.5 * LRU_C * math.log2(math.e)) * softplus
        a = jnp.exp2(k * tr + k)
        rest = 1.0 - a * a
        mult = jnp.where(rest > 0.0, rest * lax.rsqrt(rest), 0.0)
        b = mult * (ti * xh + xh)
        a_scr[d, :, :, sl] = a.reshape(n, BATCH, LRU_HALF)
        b_scr[d, :, :, sl] = b.reshape(n, BATCH, LRU_HALF)


def _lru_scan(n, hf0, hb0, a_scr, b_scr, write_f, write_b):
    def body(r, carry):
        hf, hb = carry
        rb = n - 1 - r
        hf = a_scr[0, r] * hf + b_scr[0, r]
        hb = a_scr[1, rb] * hb + b_scr[1, rb]
        write_f(r, hf)
        write_b(rb, hb)
        return hf, hb
    return lax.fori_loop(0, n, body, (hf0, hb0), unroll=4)


def _lru_kernel(n_runs, n_aliased, xf_ref, xfp_ref, xfn_ref, xb_ref, xbp_ref, xbn_ref,
                cw_ref, cb_ref, wg_ref, bg_ref, lam_ref, st0_ref, *refs):
    hf_ref, hb_ref, st_ref, a_scr, b_scr = refs[n_aliased:]
    i = pl.program_id(0)
    n = LRU_RUN

    @pl.when(i == 0)
    def _():
        st_ref[...] = st0_ref[...]

    def run(x_ref, prev_ref, next_ref, w):
        has_prev = (w > 0).astype(F32)
        has_next = (w < n_runs - 1).astype(F32)
        xe = jnp.concatenate(
            [prev_ref[...] * has_prev, x_ref[...], next_ref[...] * has_next], axis=0)
        return _lru_conv(xe, n, cw_ref, cb_ref)

    _lru_coeffs(run(xf_ref, xfp_ref, xfn_ref, i), 0, wg_ref, bg_ref, lam_ref, a_scr, b_scr)
    _lru_coeffs(run(xb_ref, xbp_ref, xbn_ref, n_runs - 1 - i), 1, wg_ref, bg_ref, lam_ref,
                a_scr, b_scr)

    def write_f(r, h):
        hf_ref[r] = h

    def write_b(r, h):
        hb_ref[r] = h

    hf, hb = _lru_scan(n, st_ref[0], st_ref[1], a_scr, b_scr, write_f, write_b)
    st_ref[0] = hf
    st_ref[1] = hb


def _lru_call(name, l, n_runs, run_spec, prev_spec, next_spec, x4, weights, st0, hf_in, hb_in):
    full = jax.ShapeDtypeStruct((N_BLK, TB, BATCH, D_LRU), F32)
    st_spec = pl.BlockSpec((2, BATCH, D_LRU), lambda i: (0, 0, 0))
    wspecs = [
        _layer_spec(l, (CONV_W, D_LRU)),
        _layer_spec(l, (1, D_LRU)),
        _layer_spec(l, (2, 2, LRU_HALF, 2 * LRU_HALF)),
        _layer_spec(l, (2, 2, D_LRU)),
        _layer_spec(l, (2, D_LRU)),
    ]
    fw = lambda i: i
    bw = lambda i: n_runs - 1 - i
    aliased = [] if hf_in is None else [hf_in, hb_in]
    n_in = 6 + len(wspecs) + 1
    return pl.pallas_call(
        functools.partial(_lru_kernel, n_runs, len(aliased)),
        grid=(n_runs,),
        in_specs=[run_spec(fw), prev_spec(fw), next_spec(fw),
                  run_spec(bw), prev_spec(bw), next_spec(bw)] + wspecs
        + [st_spec] + [pl.BlockSpec(memory_space=pl.ANY)] * len(aliased),
        out_specs=[run_spec(fw), run_spec(bw), st_spec],
        out_shape=[full, full, jax.ShapeDtypeStruct((2, BATCH, D_LRU), F32)],
        scratch_shapes=[pltpu.VMEM((2, LRU_RUN, BATCH, D_LRU), F32)] * 2,
        input_output_aliases={n_in + k: k for k in range(len(aliased))},
        compiler_params=_params(("arbitrary",)),
        name=name,
    )(x4, x4, x4, x4, x4, x4, *weights, st0, *aliased)


def _lru(l, xr, weights):
    x4 = xr.reshape(N_BLK, TB, BATCH, D_LRU)
    st0 = jnp.zeros((2, BATCH, D_LRU), F32)

    c0 = N_LAT_BLK
    clip = lambda k, n: jnp.clip(k, 0, n - 1)
    ctx_run = lambda f: pl.BlockSpec((None, TB, BATCH, D_LRU), lambda i: (c0 + f(i), 0, 0, 0))
    ctx_prev = lambda f: pl.BlockSpec(
        (None, CONV_LEFT, BATCH, D_LRU),
        lambda i: (c0 + clip(f(i) - 1, N_CTX_BLK), TB // CONV_LEFT - 1, 0, 0))
    ctx_next = lambda f: pl.BlockSpec(
        (None, 1, BATCH, D_LRU), lambda i: (c0 + clip(f(i) + 1, N_CTX_BLK), 0, 0, 0))
    hf, hb, st = _lru_call("lru_ctx", l, N_CTX_BLK, ctx_run, ctx_prev, ctx_next,
                           x4, weights, st0, None, None)

    lat_run = lambda f: pl.BlockSpec((GRID_H, None, BATCH, D_LRU), lambda i: (0, f(i), 0, 0))
    lat_prev = lambda f: pl.BlockSpec(
        (CONV_LEFT, None, BATCH, D_LRU),
        lambda i: (GRID_H // CONV_LEFT - 1, clip(f(i) - 1, GRID_W), 0, 0))
    lat_next = lambda f: pl.BlockSpec(
        (1, None, BATCH, D_LRU), lambda i: (0, clip(f(i) + 1, GRID_W), 0, 0))
    hf, hb, _ = _lru_call("lru_lat", l, GRID_W, lat_run, lat_prev, lat_next,
                          x4, weights, st, hf, hb)
    return hf.reshape(ROWS_ALL, D_LRU), hb.reshape(ROWS_ALL, D_LRU)


def _tail_kernel(first, has_next, *refs):
    if first:
        x_in = _input_rows(refs[0], refs[1])
        load_x = lambda rows: x_in[rows]
        refs = refs[2:]
    else:
        x_ref = refs[0]
        load_x = lambda rows: x_ref[rows, :]
        refs = refs[1:]
    (u_ref, yf_ref, yb_ref, gr_ref, hf_ref, hb_ref, mod_ref,
     d_ref, wglu_ref, bglu_ref, wout_ref, gains_ref, wfin_ref, wfout_ref) = refs[:14]
    refs = refs[14:]
    mod = mod_ref[0]
    chunk = lambda k: mod[:, k * D_MODEL:(k + 1) * D_MODEL]
    gate1, shift2, scale2, gate2 = chunk(2), chunk(3), chunk(4), chunk(5)
    groups = [slice(s * TAIL_ROWS, (s + 1) * TAIL_ROWS) for s in range(TAIL_SPLIT)]

    def mixer_out(rows):
        f32 = lambda ref: ref[rows, :].astype(F32)
        ys = _gelu(d_ref[...] * f32(u_ref) + f32(yf_ref) + f32(yb_ref))
        z = jnp.dot(ys.astype(BF16), wglu_ref[...], preferred_element_type=F32) + bglu_ref[...]
        y_s5 = ys * _sigmoid(z)
        y_lru = (f32(hf_ref) + f32(hb_ref)) * _gelu(f32(gr_ref))
        out = jnp.dot(y_s5.astype(BF16), wout_ref[0:D_S5, :], preferred_element_type=F32)
        out = out + jnp.dot(y_lru.astype(BF16), wout_ref[D_S5:, :], preferred_element_type=F32)
        return load_x(rows) + _per_batch(_rms(out, gains_ref[1:2, :]), lambda t: t * gate1[None])

    def ffn(x):
        h = _per_batch(_rms(x, gains_ref[2:3, :]), lambda t: t * (1.0 + scale2)[None] + shift2[None])
        h = h.astype(BF16)
        f = jnp.zeros((TAIL_ROWS, D_MODEL), F32)
        for start, size in FF_CHUNKS:
            gt = jnp.dot(h, wfin_ref[:, start:start + size], preferred_element_type=F32)
            up = jnp.dot(h, wfin_ref[:, D_FF + start:D_FF + start + size], preferred_element_type=F32)
            act = (gt * _sigmoid(gt) * up).astype(BF16)
            f = f + jnp.dot(act, wfout_ref[start:start + size, :], preferred_element_type=F32)
        return x + _per_batch(_rms(f, gains_ref[3:4, :]), lambda t: t * gate2[None])

    xs = [mixer_out(rows) for rows in groups]
    xs = [ffn(x) for x in xs]
    for s, (rows, x) in enumerate(zip(groups, xs)):
        if has_next:
            modn_ref, gn_ref, win_ref, o_ref, un_ref, xrn_ref, grn_ref = refs
            o_ref[rows, :] = x
            _premix_math(x, modn_ref[0], gn_ref[0:1, :], win_ref,
                         un_ref.at[rows, :], xrn_ref.at[rows, :], grn_ref.at[rows, :])
        else:
            (o_ref,) = refs
            steps = slice(s * TAIL_ROWS // BATCH, (s + 1) * TAIL_ROWS // BATCH)
            o_ref[:, steps, :] = jnp.swapaxes(x.reshape(TAIL_ROWS // BATCH, BATCH, D_MODEL), 0, 1)


def _tail(l, x_src, u, yf, yb, gr, hf, hb, mod, d, wglu, bglu, wout, gains, wfin, wfout, w_in):
    first = len(x_src) == 2
    has_next = l < DEPTH - 1
    n_blk = N_BLK if has_next else N_LAT_BLK
    rows = n_blk * TM
    row = lambda i: (i, 0)
    once = pl.Buffered(1)
    wide = pl.BlockSpec((TM, D_MODEL), row)
    half = pl.BlockSpec((TM, D_S5), row)
    in_specs = ([_X_SPEC, _CTX_SPEC] if first else [wide]) + [half] * 6 + [
        _mod_spec(l),
        _layer_spec(l, (1, D_S5)),
        _layer_spec(l, (D_S5, D_S5), pipeline_mode=once),
        _layer_spec(l, (1, D_S5)),
        _layer_spec(l, (D_MODEL, D_MODEL), pipeline_mode=once),
        _layer_spec(l, (4, D_MODEL)),
        _layer_spec(l, (D_MODEL, 2 * D_FF), pipeline_mode=once),
        _layer_spec(l, (D_FF, D_MODEL), pipeline_mode=once),
    ]
    args = [*x_src, u, yf, yb, gr, hf, hb, mod, d, wglu, bglu, wout, gains, wfin, wfout]
    if has_next:
        in_specs += [_mod_spec(l + 1), _layer_spec(l + 1, (4, D_MODEL)),
                     _layer_spec(l + 1, (D_MODEL, D_IN), pipeline_mode=once)]
        args += [mod, gains, w_in]
        out_specs = [wide] + [half] * 3
        out_shape = [jax.ShapeDtypeStruct((rows, D_MODEL), F32)]
        out_shape += [jax.ShapeDtypeStruct((rows, D_S5), dt) for dt in (BF16, F32, BF16)]
    else:
        out_specs = [pl.BlockSpec((BATCH, TB, D_MODEL), lambda i: (0, i, 0))]
        out_shape = [jax.ShapeDtypeStruct((BATCH, SEQ, D_MODEL), F32)]
    return pl.pallas_call(
        functools.partial(_tail_kernel, first, has_next),
        grid=(n_blk,),
        in_specs=in_specs,
        out_specs=out_specs,
        out_shape=out_shape,
        compiler_params=_params(("parallel",)),
        name="tail",
    )(*args)


def _s5_matrices(a_re, a_im, log_dt, b_re, b_im, c_re, c_im):
    dt = jnp.exp(log_dt)[..., None]
    mag = jnp.exp(a_re * dt)
    ab_re = mag * jnp.cos(a_im * dt)
    ab_im = mag * jnp.sin(a_im * dt)
    den = a_re * a_re + a_im * a_im
    nr = ab_re - 1.0
    f_re = (nr * a_re + ab_im * a_im) / den
    f_im = (ab_im * a_re - nr * a_im) / den
    bb_re = f_re[..., None] * b_re - f_im[..., None] * b_im
    bb_im = f_re[..., None] * b_im + f_im[..., None] * b_re
    eye = jnp.eye(GROUPS_PER_SLAB, dtype=F32)

    def in_map(m):
        m = m.reshape(DEPTH, 2, N_SLAB, GROUPS_PER_SLAB, S5_STATE, S5_GROUP)
        return jnp.einsum('ldjgnp,gh->ldjgphn', m, eye).reshape(DEPTH, 2, N_SLAB, LANES, SLAB_STATE)

    def out_map(m):
        m = m.reshape(DEPTH, 2, N_SLAB, GROUPS_PER_SLAB, S5_GROUP, S5_STATE)
        return jnp.einsum('ldjgpn,gh->ldjgnhp', m, eye).reshape(DEPTH, 2, N_SLAB, SLAB_STATE, LANES)

    col_tiles = lambda m: m.reshape(DEPTH, 2, N_SLAB, LANES, S5_TILES, LANES).transpose(0, 1, 2, 4, 3, 5)
    row_tiles = lambda m: m.reshape(DEPTH, 2, N_SLAB, S5_TILES, LANES, LANES)
    bb = jnp.concatenate([col_tiles(in_map(bb_re)), col_tiles(in_map(bb_im))], axis=-1).astype(BF16)
    cc = jnp.concatenate([row_tiles(out_map(c_re)), row_tiles(out_map(-c_im))], axis=-2).astype(BF16)
    tiles = lambda m: m.reshape(DEPTH, 2, N_SLAB, S5_TILES, LANES)
    a = jnp.stack([tiles(ab_re), tiles(ab_im)], axis=4)
    a = jnp.broadcast_to(a[..., None, :], (DEPTH, 2, N_SLAB, S5_TILES, 2, BATCH, LANES))
    return bb, cc, a


def _lru_gate_matrix(w_rg, w_ig):
    heads = LRU_HALF // LRU_HEAD_DIM
    halves = D_LRU // LRU_HALF
    eye = jnp.eye(heads, dtype=F32)

    def bd(w):
        w = w.reshape(DEPTH, 2, halves, heads, LRU_HEAD_DIM, LRU_HEAD_DIM)
        return jnp.einsum('ldbhij,hk->ldbhikj', w, eye).reshape(DEPTH, 2, halves, LRU_HALF, LRU_HALF)

    return jnp.concatenate([bd(w_rg), bd(w_ig)], axis=-1).astype(BF16)


def kernel(x, c, ctx, c_ctx, w_ada, b_ada, norm_gains, w_in, s5_a_re, s5_a_im, s5_log_dt, s5_b_re, s5_b_im, s5_c_re, s5_c_im, s5_d, s5_w_glu, s5_b_glu, lru_conv_w, lru_conv_b, lru_w_rg, lru_b_rg, lru_w_ig, lru_b_ig, lru_lambda, w_out, w_ffn_in, w_ffn_out):
    assert x.shape == (BATCH, SEQ, D_MODEL) and ctx.shape == (BATCH, CTX_LEN, D_MODEL)
    cond = jnp.concatenate([c, c_ctx[None], jnp.zeros((2 * BATCH - BATCH - 1, D_MODEL), F32)], axis=0)
    m = _adaln(cond, w_ada, b_ada)
    mod = jnp.stack(
        [m[:, :BATCH], jnp.broadcast_to(m[:, BATCH:BATCH + 1], (DEPTH, BATCH, 6 * D_MODEL))], axis=1)

    w_in, w_glu, w_out, w_ffn_in, w_ffn_out = (
        w.astype(BF16) for w in (w_in, s5_w_glu, w_out, w_ffn_in, w_ffn_out))
    s5_w = _s5_matrices(s5_a_re, s5_a_im, s5_log_dt, s5_b_re, s5_b_im, s5_c_re, s5_c_im)
    lru_w = (0.5 * lru_conv_w, 0.5 * lru_conv_b[:, None], _lru_gate_matrix(lru_w_rg, lru_w_ig),
             0.5 * jnp.stack([lru_b_rg, lru_b_ig], axis=2), lru_lambda)

    u, xr, gr = _premix(0, x, ctx, mod, norm_gains, w_in)
    x_src = (x, ctx)
    for l in range(DEPTH):
        yf, yb = _s5(l, u, *s5_w)
        hf, hb = _lru(l, xr, lru_w)
        outs = _tail(l, x_src, u, yf, yb, gr, hf, hb, mod, s5_d[:, None], w_glu, s5_b_glu[:, None],
                     w_out, norm_gains, w_ffn_in, w_ffn_out, w_in)
        if l < DEPTH - 1:
            x_all, u, xr, gr = outs
            x_src = (x_all,)
    return outs[0]
```

```python
import functools
import math

import jax
import jax.numpy as jnp
from jax import lax
from jax.experimental import pallas as pl
from jax.experimental.pallas import tpu as pltpu

F32 = jnp.float32
BF16 = jnp.bfloat16

D_MODEL = 1024
BATCH = 8
SEQ = 4096
DEPTH = 4
GRID_W = 64
GRID_H = SEQ // GRID_W
CTX_LEN = 256
D_S5 = 512
S5_GROUP = 16
S5_GROUPS = 32
S5_STATE = 64
D_LRU = 512
LRU_HEADS = 8
LRU_HEAD_DIM = 64
CONV_W = 4
CONV_LEFT = 2
LRU_C = 8.0
D_IN = D_S5 + 2 * D_LRU
D_FF = 2816
EPS = 1e-6

LANES = 128
SUBLANES = 8
TB = 64
TM = TB * BATCH
N_LAT_BLK = SEQ // TB
N_CTX_BLK = CTX_LEN // TB
N_BLK = N_LAT_BLK + N_CTX_BLK
ROWS_LAT = SEQ * BATCH
ROWS_ALL = (SEQ + CTX_LEN) * BATCH
N_SLAB = D_S5 // LANES
GROUPS_PER_SLAB = LANES // S5_GROUP
SLAB_STATE = GROUPS_PER_SLAB * S5_STATE
S5_TILES = SLAB_STATE // LANES
S5_TB = 128
S5_TM = S5_TB * BATCH
S5_N_LAT = SEQ // S5_TB
S5_N_CTX = CTX_LEN // S5_TB
S5_N_BLK = S5_N_LAT + S5_N_CTX
LRU_HALF = 256
LRU_RUN = GRID_H
assert TB == LRU_RUN
FF_CHUNKS = ((0, 1024), (1024, 1024), (2048, 768))
TAIL_SPLIT = 2
TAIL_ROWS = TM // TAIL_SPLIT
VMEM_LIMIT = 58 * 1024 * 1024


def _params(sem):
    return pltpu.CompilerParams(dimension_semantics=sem, vmem_limit_bytes=VMEM_LIMIT)


def _rms(x, g):
    var = jnp.mean(x * x, axis=-1, keepdims=True)
    return x * lax.rsqrt(var + EPS) * g


def _per_batch(x, fn):
    r, ch = x.shape
    return fn(x.reshape(r // BATCH, BATCH, ch)).reshape(r, ch)


def _gelu(x):
    return jax.nn.gelu(x, approximate=True)


def _sigmoid(x):
    return 0.5 * jnp.tanh(0.5 * x) + 0.5


def _adaln_kernel(c_ref, w_ref, b_ref, o_ref):
    c = c_ref[...]
    s = (c * jax.nn.sigmoid(c)).astype(BF16)
    o_ref[0] = jnp.dot(s, w_ref[0].astype(BF16), preferred_element_type=F32) + b_ref[0]


def _adaln(cond, w_ada, b_ada):
    rows = cond.shape[0]
    nb = 6
    return pl.pallas_call(
        _adaln_kernel,
        grid=(DEPTH, nb),
        in_specs=[
            pl.BlockSpec((rows, D_MODEL), lambda l, n: (0, 0)),
            pl.BlockSpec((1, D_MODEL, D_MODEL), lambda l, n: (l, 0, n)),
            pl.BlockSpec((1, 1, D_MODEL), lambda l, n: (l, 0, n)),
        ],
        out_specs=pl.BlockSpec((1, rows, D_MODEL), lambda l, n: (l, 0, n)),
        out_shape=jax.ShapeDtypeStruct((DEPTH, rows, 6 * D_MODEL), F32),
        compiler_params=_params(("arbitrary", "arbitrary")),
        name="adaln",
    )(cond, w_ada, b_ada.reshape(DEPTH, 1, 6 * D_MODEL))


def _mod_index(i):
    return (i >= N_LAT_BLK).astype(jnp.int32)


def _layer_spec(l, shape, **kw):
    zeros = (0,) * len(shape)
    return pl.BlockSpec((None,) + tuple(shape), lambda i: (l,) + zeros, **kw)


def _mod_spec(l):
    return pl.BlockSpec((None, 1, BATCH, 6 * D_MODEL), lambda i: (l, _mod_index(i), 0, 0))


def _premix_math(x, mod, g, w_ref, u_ref, xr_ref, gr_ref):
    shift, scale = mod[:, 0:D_MODEL], mod[:, D_MODEL:2 * D_MODEL]
    h = _per_batch(_rms(x, g), lambda t: t * (1.0 + scale)[None] + shift[None])
    p = jnp.dot(h.astype(BF16), w_ref[...], preferred_element_type=F32)
    u_ref[...] = p[:, 0:D_S5].astype(u_ref.dtype)
    xr_ref[...] = p[:, D_S5:D_S5 + D_LRU]
    gr_ref[...] = p[:, D_S5 + D_LRU:].astype(gr_ref.dtype)


def _input_rows(x_ref, ctx_ref):
    blk = jnp.where(pl.program_id(0) < N_LAT_BLK, x_ref[...], ctx_ref[...])
    return jnp.swapaxes(blk, 0, 1).reshape(TM, D_MODEL)


_X_SPEC = pl.BlockSpec((BATCH, TB, D_MODEL), lambda i: (0, jnp.minimum(i, N_LAT_BLK - 1), 0))
_CTX_SPEC = pl.BlockSpec((BATCH, TB, D_MODEL), lambda i: (0, jnp.maximum(i - N_LAT_BLK, 0), 0))


def _premix_kernel(x_ref, ctx_ref, mod_ref, g_ref, w_ref, u_ref, xr_ref, gr_ref):
    _premix_math(_input_rows(x_ref, ctx_ref), mod_ref[0], g_ref[0:1, :], w_ref, u_ref, xr_ref, gr_ref)


def _premix(l, x, ctx, mod, gains, w_in):
    row = lambda i: (i, 0)
    half = lambda dtype: jax.ShapeDtypeStruct((ROWS_ALL, D_S5), dtype)
    return pl.pallas_call(
        _premix_kernel,
        grid=(N_BLK,),
        in_specs=[
            _X_SPEC,
            _CTX_SPEC,
            _mod_spec(l),
            _layer_spec(l, (4, D_MODEL)),
            _layer_spec(l, (D_MODEL, D_IN)),
        ],
        out_specs=[pl.BlockSpec((TM, D_S5), row)] * 3,
        out_shape=[half(BF16), half(F32), half(BF16)],
        compiler_params=_params(("parallel",)),
        name="premix",
    )(x, ctx, mod, gains, w_in)


def _s5_kernel(uf_ref, ub_ref, win_ref, wout_ref, wd_ref, a2_ref, yf_ref, yb_ref, st_scr, carry_scr):
    @pl.when(pl.program_id(0) == 0)
    def _():
        st_scr[...] = jnp.zeros_like(st_scr)
        carry_scr[...] = jnp.zeros_like(carry_scr)

    u_refs = (uf_ref, ub_ref)
    y_refs = (yf_ref, yb_ref)
    np_ = S5_TB // 2
    dot = lambda p, q: lax.dot_general(p, q, (((1,), (0,)), ((), ())), preferred_element_type=F32)
    for j in range(N_SLAB):
        lanes = slice(j * LANES, (j + 1) * LANES)
        for d in range(2):
            u = u_refs[d][:, lanes].astype(F32).reshape(np_, 2, BATCH, LANES)
            even, odd = u[:, 0].reshape(np_ * BATCH, LANES), u[:, 1].reshape(np_ * BATCH, LANES)
            lhs = jnp.concatenate([even, odd] if d == 0 else [odd, even], axis=1)
            out = None
            for m in range(S5_TILES):
                z = dot(lhs, win_ref[d, j, m])
                ar, ai = a2_ref[d, j, m, 0], a2_ref[d, j, m, 1]
                sr, si = st_scr[d, j, m, 0], st_scr[d, j, m, 1]
                states = [None] * np_
                for p in range(np_):
                    r = np_ - 1 - p if d == 1 else p
                    zr = z[r * BATCH:(r + 1) * BATCH]
                    sr, si = (ar * sr - ai * si + zr[:, 0:LANES],
                              ar * si + ai * sr + zr[:, LANES:])
                    states[r] = jnp.concatenate([sr, si], axis=1)
                st_scr[d, j, m, 0] = sr
                st_scr[d, j, m, 1] = si
                part = dot(jnp.concatenate(states, axis=0), wout_ref[d, j, m])
                out = part if out is None else out + part
            direct, ahead = out[:, 0:LANES], out[:, LANES:]
            carry = carry_scr[d, j]
            if d == 0:
                carry_scr[d, j] = ahead[(np_ - 1) * BATCH:]
                y_even = jnp.concatenate([carry, ahead[:(np_ - 1) * BATCH]], axis=0) + dot(even, wd_ref[d, j])
                y_odd = direct
            else:
                carry_scr[d, j] = ahead[:BATCH]
                y_odd = jnp.concatenate([ahead[BATCH:], carry], axis=0) + dot(odd, wd_ref[d, j])
                y_even = direct
            y = jnp.stack([y_even.reshape(np_, BATCH, LANES), y_odd.reshape(np_, BATCH, LANES)], axis=1)
            y_refs[d][:, lanes] = y.reshape(S5_TM, LANES).astype(y_refs[d].dtype)


def _s5_fwd_block(i):
    return jnp.where(i < S5_N_CTX, S5_N_LAT + i, i - S5_N_CTX)


def _s5_bwd_block(i):
    return S5_N_BLK - 1 - i


def _s5(l, u, w_in, w_out, w_d, a2):
    out = jax.ShapeDtypeStruct((ROWS_ALL, D_S5), BF16)
    return pl.pallas_call(
        _s5_kernel,
        grid=(S5_N_BLK,),
        in_specs=[
            pl.BlockSpec((S5_TM, D_S5), lambda i: (_s5_fwd_block(i), 0)),
            pl.BlockSpec((S5_TM, D_S5), lambda i: (_s5_bwd_block(i), 0)),
            _layer_spec(l, (2, N_SLAB, S5_TILES, 2 * LANES, 2 * LANES)),
            _layer_spec(l, (2, N_SLAB, S5_TILES, 2 * LANES, 2 * LANES)),
            _layer_spec(l, (2, N_SLAB, LANES, LANES)),
            _layer_spec(l, (2, N_SLAB, S5_TILES, 2, BATCH, LANES)),
        ],
        out_specs=[
            pl.BlockSpec((S5_TM, D_S5), lambda i: (_s5_fwd_block(i), 0)),
            pl.BlockSpec((S5_TM, D_S5), lambda i: (_s5_bwd_block(i), 0)),
        ],
        out_shape=[out, out],
        scratch_shapes=[pltpu.VMEM((2, N_SLAB, S5_TILES, 2, BATCH, LANES), F32),
                        pltpu.VMEM((2, N_SLAB, BATCH, LANES), F32)],
        compiler_params=_params(("arbitrary",)),
        name="s5_scan",
    )(u, u, w_in, w_out, w_d, a2)


def _lru_conv(xe, n, cw_ref, cb_ref):
    out = xe[0:n] * cw_ref[0:1, :][None]
    for k in range(1, CONV_W):
        out = out + xe[k:k + n] * cw_ref[k:k + 1, :][None]
    return out + cb_ref[...][None]


def _lru_coeffs(xc, d, wg_ref, bg_ref, lam_ref, a_scr, b_scr):
    n = xc.shape[0]
    x2 = xc.reshape(n * BATCH, D_LRU)
    for hb in range(D_LRU // LRU_HALF):
        sl = slice(hb * LRU_HALF, (hb + 1) * LRU_HALF)
        xh = x2[:, sl]
        g = jnp.dot(xh.astype(BF16), wg_ref[d, hb], preferred_element_type=F32)
        tr = jnp.tanh(g[:, 0:LRU_HALF] + bg_ref[d, 0:1, sl])
        ti = jnp.tanh(g[:, LRU_HALF:] + bg_ref[d, 1:2, sl])
        lam = lam_ref[d:d + 1, sl]
        softplus = jnp.maximum(-lam, 0.0) + jnp.log1p(jnp.exp(-jnp.abs(lam)))
        k = (-0.5 * LRU_C * math.log2(math.e)) * softplus
        a = jnp.exp2(k * tr + k)
        rest = 1.0 - a * a
        mult = jnp.where(rest > 0.0, rest * lax.rsqrt(rest), 0.0)
        b = mult * (ti * xh + xh)
        a_scr[d, :, :, sl] = a.reshape(n, BATCH, LRU_HALF)
        b_scr[d, :, :, sl] = b.reshape(n, BATCH, LRU_HALF)


def _lru_scan(n, hf0, hb0, a_scr, b_scr, write_f, write_b):
    def body(r, carry):
        hf, hb = carry
        rb = n - 1 - r
        hf = a_scr[0, r] * hf + b_scr[0, r]
        hb = a_scr[1, rb] * hb + b_scr[1, rb]
        write_f(r, hf)
        write_b(rb, hb)
        return hf, hb
    return lax.fori_loop(0, n, body, (hf0, hb0), unroll=4)


def _lru_kernel(n_runs, n_aliased, xf_ref, xfp_ref, xfn_ref, xb_ref, xbp_ref, xbn_ref,
                cw_ref, cb_ref, wg_ref, bg_ref, lam_ref, st0_ref, *refs):
    hf_ref, hb_ref, st_ref, a_scr, b_scr = refs[n_aliased:]
    i = pl.program_id(0)
    n = LRU_RUN

    @pl.when(i == 0)
    def _():
        st_ref[...] = st0_ref[...]

    def run(x_ref, prev_ref, next_ref, w):
        has_prev = (w > 0).astype(F32)
        has_next = (w < n_runs - 1).astype(F32)
        xe = jnp.concatenate(
            [prev_ref[...] * has_prev, x_ref[...], next_ref[...] * has_next], axis=0)
        return _lru_conv(xe, n, cw_ref, cb_ref)

    _lru_coeffs(run(xf_ref, xfp_ref, xfn_ref, i), 0, wg_ref, bg_ref, lam_ref, a_scr, b_scr)
    _lru_coeffs(run(xb_ref, xbp_ref, xbn_ref, n_runs - 1 - i), 1, wg_ref, bg_ref, lam_ref,
                a_scr, b_scr)

    def write_f(r, h):
        hf_ref[r] = h

    def write_b(r, h):
        hb_ref[r] = h

    hf, hb = _lru_scan(n, st_ref[0], st_ref[1], a_scr, b_scr, write_f, write_b)
    st_ref[0] = hf
    st_ref[1] = hb


def _lru_call(name, l, n_runs, run_spec, prev_spec, next_spec, x4, weights, st0, hf_in, hb_in):
    full = jax.ShapeDtypeStruct((N_BLK, TB, BATCH, D_LRU), F32)
    st_spec = pl.BlockSpec((2, BATCH, D_LRU), lambda i: (0, 0, 0))
    wspecs = [
        _layer_spec(l, (CONV_W, D_LRU)),
        _layer_spec(l, (1, D_LRU)),
        _layer_spec(l, (2, 2, LRU_HALF, 2 * LRU_HALF)),
        _layer_spec(l, (2, 2, D_LRU)),
        _layer_spec(l, (2, D_LRU)),
    ]
    fw = lambda i: i
    bw = lambda i: n_runs - 1 - i
    aliased = [] if hf_in is None else [hf_in, hb_in]
    n_in = 6 + len(wspecs) + 1
    return pl.pallas_call(
        functools.partial(_lru_kernel, n_runs, len(aliased)),
        grid=(n_runs,),
        in_specs=[run_spec(fw), prev_spec(fw), next_spec(fw),
                  run_spec(bw), prev_spec(bw), next_spec(bw)] + wspecs
        + [st_spec] + [pl.BlockSpec(memory_space=pl.ANY)] * len(aliased),
        out_specs=[run_spec(fw), run_spec(bw), st_spec],
        out_shape=[full, full, jax.ShapeDtypeStruct((2, BATCH, D_LRU), F32)],
        scratch_shapes=[pltpu.VMEM((2, LRU_RUN, BATCH, D_LRU), F32)] * 2,
        input_output_aliases={n_in + k: k for k in range(len(aliased))},
        compiler_params=_params(("arbitrary",)),
        name=name,
    )(x4, x4, x4, x4, x4, x4, *weights, st0, *aliased)


def _lru(l, xr, weights):
    x4 = xr.reshape(N_BLK, TB, BATCH, D_LRU)
    st0 = jnp.zeros((2, BATCH, D_LRU), F32)

    c0 = N_LAT_BLK
    clip = lambda k, n: jnp.clip(k, 0, n - 1)
    ctx_run = lambda f: pl.BlockSpec((None, TB, BATCH, D_LRU), lambda i: (c0 + f(i), 0, 0, 0))
    ctx_prev = lambda f: pl.BlockSpec(
        (None, CONV_LEFT, BATCH, D_LRU),
        lambda i: (c0 + clip(f(i) - 1, N_CTX_BLK), TB // CONV_LEFT - 1, 0, 0))
    ctx_next = lambda f: pl.BlockSpec(
        (None, 1, BATCH, D_LRU), lambda i: (c0 + clip(f(i) + 1, N_CTX_BLK), 0, 0, 0))
    hf, hb, st = _lru_call("lru_ctx", l, N_CTX_BLK, ctx_run, ctx_prev, ctx_next,
                           x4, weights, st0, None, None)

    lat_run = lambda f: pl.BlockSpec((GRID_H, None, BATCH, D_LRU), lambda i: (0, f(i), 0, 0))
    lat_prev = lambda f: pl.BlockSpec(
        (CONV_LEFT, None, BATCH, D_LRU),
        lambda i: (GRID_H // CONV_LEFT - 1, clip(f(i) - 1, GRID_W), 0, 0))
    lat_next = lambda f: pl.BlockSpec(
        (1, None, BATCH, D_LRU), lambda i: (0, clip(f(i) + 1, GRID_W), 0, 0))
    hf, hb, _ = _lru_call("lru_lat", l, GRID_W, lat_run, lat_prev, lat_next,
                          x4, weights, st, hf, hb)
    return hf.reshape(ROWS_ALL, D_LRU), hb.reshape(ROWS_ALL, D_LRU)


def _tail_kernel(first, has_next, *refs):
    if first:
        x_in = _input_rows(refs[0], refs[1])
        load_x = lambda rows: x_in[rows]
        refs = refs[2:]
    else:
        x_ref = refs[0]
        load_x = lambda rows: x_ref[rows, :]
        refs = refs[1:]
    (u_ref, yf_ref, yb_ref, gr_ref, hf_ref, hb_ref, mod_ref,
     d_ref, wglu_ref, bglu_ref, wout_ref, gains_ref, wfin_ref, wfout_ref) = refs[:14]
    refs = refs[14:]
    mod = mod_ref[0]
    chunk = lambda k: mod[:, k * D_MODEL:(k + 1) * D_MODEL]
    gate1, shift2, scale2, gate2 = chunk(2), chunk(3), chunk(4), chunk(5)
    groups = [slice(s * TAIL_ROWS, (s + 1) * TAIL_ROWS) for s in range(TAIL_SPLIT)]

    def mixer_out(rows):
        f32 = lambda ref: ref[rows, :].astype(F32)
        ys = _gelu(d_ref[...] * f32(u_ref) + f32(yf_ref) + f32(yb_ref))
        z = jnp.dot(ys.astype(BF16), wglu_ref[...], preferred_element_type=F32) + bglu_ref[...]
        y_s5 = ys * _sigmoid(z)
        y_lru = (f32(hf_ref) + f32(hb_ref)) * _gelu(f32(gr_ref))
        out = jnp.dot(y_s5.astype(BF16), wout_ref[0:D_S5, :], preferred_element_type=F32)
        out = out + jnp.dot(y_lru.astype(BF16), wout_ref[D_S5:, :], preferred_element_type=F32)
        return load_x(rows) + _per_batch(_rms(out, gains_ref[1:2, :]), lambda t: t * gate1[None])

    def ffn(x):
        h = _per_batch(_rms(x, gains_ref[2:3, :]), lambda t: t * (1.0 + scale2)[None] + shift2[None])
        h = h.astype(BF16)
        f = jnp.zeros((TAIL_ROWS, D_MODEL), F32)
        for start, size in FF_CHUNKS:
            gt = jnp.dot(h, wfin_ref[:, start:start + size], preferred_element_type=F32)
            up = jnp.dot(h, wfin_ref[:, D_FF + start:D_FF + start + size], preferred_element_type=F32)
            act = (gt * _sigmoid(gt) * up).astype(BF16)
            f = f + jnp.dot(act, wfout_ref[start:start + size, :], preferred_element_type=F32)
        return x + _per_batch(_rms(f, gains_ref[3:4, :]), lambda t: t * gate2[None])

    xs = [mixer_out(rows) for rows in groups]
    xs = [ffn(x) for x in xs]
    for s, (rows, x) in enumerate(zip(groups, xs)):
        if has_next:
            modn_ref, gn_ref, win_ref, o_ref, un_ref, xrn_ref, grn_ref = refs
            o_ref[rows, :] = x
            _premix_math(x, modn_ref[0], gn_ref[0:1, :], win_ref,
                         un_ref.at[rows, :], xrn_ref.at[rows, :], grn_ref.at[rows, :])
        else:
            (o_ref,) = refs
            steps = slice(s * TAIL_ROWS // BATCH, (s + 1) * TAIL_ROWS // BATCH)
            o_ref[:, steps, :] = jnp.swapaxes(x.reshape(TAIL_ROWS // BATCH, BATCH, D_MODEL), 0, 1)


def _tail(l, x_src, u, yf, yb, gr, hf, hb, mod, d, wglu, bglu, wout, gains, wfin, wfout, w_in):
    first = len(x_src) == 2
    has_next = l < DEPTH - 1
    n_blk = N_BLK if has_next else N_LAT_BLK
    rows = n_blk * TM
    row = lambda i: (i, 0)
    once = pl.Buffered(1)
    wide = pl.BlockSpec((TM, D_MODEL), row)
    half = pl.BlockSpec((TM, D_S5), row)
    in_specs = ([_X_SPEC, _CTX_SPEC] if first else [wide]) + [half] * 6 + [
        _mod_spec(l),
        _layer_spec(l, (1, D_S5)),
        _layer_spec(l, (D_S5, D_S5), pipeline_mode=once),
        _layer_spec(l, (1, D_S5)),
        _layer_spec(l, (D_MODEL, D_MODEL), pipeline_mode=once),
        _layer_spec(l, (4, D_MODEL)),
        _layer_spec(l, (D_MODEL, 2 * D_FF), pipeline_mode=once),
        _layer_spec(l, (D_FF, D_MODEL), pipeline_mode=once),
    ]
    args = [*x_src, u, yf, yb, gr, hf, hb, mod, d, wglu, bglu, wout, gains, wfin, wfout]
    if has_next:
        in_specs += [_mod_spec(l + 1), _layer_spec(l + 1, (4, D_MODEL)),
                     _layer_spec(l + 1, (D_MODEL, D_IN), pipeline_mode=once)]
        args += [mod, gains, w_in]
        out_specs = [wide] + [half] * 3
        out_shape = [jax.ShapeDtypeStruct((rows, D_MODEL), F32)]
        out_shape += [jax.ShapeDtypeStruct((rows, D_S5), dt) for dt in (BF16, F32, BF16)]
    else:
        out_specs = [pl.BlockSpec((BATCH, TB, D_MODEL), lambda i: (0, i, 0))]
        out_shape = [jax.ShapeDtypeStruct((BATCH, SEQ, D_MODEL), F32)]
    return pl.pallas_call(
        functools.partial(_tail_kernel, first, has_next),
        grid=(n_blk,),
        in_specs=in_specs,
        out_specs=out_specs,
        out_shape=out_shape,
        compiler_params=_params(("parallel",)),
        name="tail",
    )(*args)


def _s5_matrices(a_re, a_im, log_dt, b_re, b_im, c_re, c_im):
    dt = jnp.exp(log_dt)[..., None]
    mag = jnp.exp(a_re * dt)
    ab_re = mag * jnp.cos(a_im * dt)
    ab_im = mag * jnp.sin(a_im * dt)
    den = a_re * a_re + a_im * a_im
    nr = ab_re - 1.0
    f_re = (nr * a_re + ab_im * a_im) / den
    f_im = (ab_im * a_re - nr * a_im) / den
    bb_re = f_re[..., None] * b_re - f_im[..., None] * b_im
    bb_im = f_re[..., None] * b_im + f_im[..., None] * b_re
    eye = jnp.eye(GROUPS_PER_SLAB, dtype=F32)

    def in_map(m):
        m = m.reshape(DEPTH, 2, N_SLAB, GROUPS_PER_SLAB, S5_STATE, S5_GROUP)
        return jnp.einsum('ldjgnp,gh->ldjgphn', m, eye).reshape(DEPTH, 2, N_SLAB, LANES, SLAB_STATE)

    def out_map(m):
        m = m.reshape(DEPTH, 2, N_SLAB, GROUPS_PER_SLAB, S5_GROUP, S5_STATE)
        return jnp.einsum('ldjgpn,gh->ldjgnhp', m, eye).reshape(DEPTH, 2, N_SLAB, SLAB_STATE, LANES)

    br, bi = in_map(bb_re), in_map(bb_im)
    cr, ci = out_map(c_re), out_map(c_im)
    ar = ab_re.reshape(DEPTH, 2, N_SLAB, 1, SLAB_STATE)
    ai = ab_im.reshape(DEPTH, 2, N_SLAB, 1, SLAB_STATE)
    art, ait = jnp.swapaxes(ar, -1, -2), jnp.swapaxes(ai, -1, -2)
    in_re = jnp.concatenate([br * ar - bi * ai, br], axis=-2)
    in_im = jnp.concatenate([br * ai + bi * ar, bi], axis=-2)
    out_re = jnp.concatenate([cr, cr * art - ci * ait], axis=-1)
    out_im = jnp.concatenate([-ci, -(cr * ait + ci * art)], axis=-1)
    hi = lax.Precision.HIGHEST
    w_d = (jnp.einsum('ldjpn,ldjnq->ldjpq', br, cr, precision=hi)
           - jnp.einsum('ldjpn,ldjnq->ldjpq', bi, ci, precision=hi)).astype(BF16)
    col_tiles = lambda m: m.reshape(DEPTH, 2, N_SLAB, 2 * LANES, S5_TILES, LANES).transpose(0, 1, 2, 4, 3, 5)
    row_tiles = lambda m: m.reshape(DEPTH, 2, N_SLAB, S5_TILES, LANES, 2 * LANES)
    w_in = jnp.concatenate([col_tiles(in_re), col_tiles(in_im)], axis=-1).astype(BF16)
    w_out = jnp.concatenate([row_tiles(out_re), row_tiles(out_im)], axis=-2).astype(BF16)
    tiles = lambda m: m.reshape(DEPTH, 2, N_SLAB, S5_TILES, LANES)
    a2 = jnp.stack([tiles(ab_re * ab_re - ab_im * ab_im), tiles(2.0 * ab_re * ab_im)], axis=4)
    a2 = jnp.broadcast_to(a2[..., None, :], (DEPTH, 2, N_SLAB, S5_TILES, 2, BATCH, LANES))
    return w_in, w_out, w_d, a2


def _lru_gate_matrix(w_rg, w_ig):
    heads = LRU_HALF // LRU_HEAD_DIM
    halves = D_LRU // LRU_HALF
    eye = jnp.eye(heads, dtype=F32)

    def bd(w):
        w = w.reshape(DEPTH, 2, halves, heads, LRU_HEAD_DIM, LRU_HEAD_DIM)
        return jnp.einsum('ldbhij,hk->ldbhikj', w, eye).reshape(DEPTH, 2, halves, LRU_HALF, LRU_HALF)

    return jnp.concatenate([bd(w_rg), bd(w_ig)], axis=-1).astype(BF16)


def kernel(x, c, ctx, c_ctx, w_ada, b_ada, norm_gains, w_in, s5_a_re, s5_a_im, s5_log_dt, s5_b_re, s5_b_im, s5_c_re, s5_c_im, s5_d, s5_w_glu, s5_b_glu, lru_conv_w, lru_conv_b, lru_w_rg, lru_b_rg, lru_w_ig, lru_b_ig, lru_lambda, w_out, w_ffn_in, w_ffn_out):
    assert x.shape == (BATCH, SEQ, D_MODEL) and ctx.shape == (BATCH, CTX_LEN, D_MODEL)
    cond = jnp.concatenate([c, c_ctx[None], jnp.zeros((2 * BATCH - BATCH - 1, D_MODEL), F32)], axis=0)
    m = _adaln(cond, w_ada, b_ada)
    mod = jnp.stack(
        [m[:, :BATCH], jnp.broadcast_to(m[:, BATCH:BATCH + 1], (DEPTH, BATCH, 6 * D_MODEL))], axis=1)

    w_in, w_glu, w_out, w_ffn_in, w_ffn_out = (
        w.astype(BF16) for w in (w_in, s5_w_glu, w_out, w_ffn_in, w_ffn_out))
    s5_w = _s5_matrices(s5_a_re, s5_a_im, s5_log_dt, s5_b_re, s5_b_im, s5_c_re, s5_c_im)
    lru_w = (0.5 * lru_conv_w, 0.5 * lru_conv_b[:, None], _lru_gate_matrix(lru_w_rg, lru_w_ig),
             0.5 * jnp.stack([lru_b_rg, lru_b_ig], axis=2), lru_lambda)

    u, xr, gr = _premix(0, x, ctx, mod, norm_gains, w_in)
    x_src = (x, ctx)
    for l in range(DEPTH):
        yf, yb = _s5(l, u, *s5_w)
        hf, hb = _lru(l, xr, lru_w)
        outs = _tail(l, x_src, u, yf, yb, gr, hf, hb, mod, s5_d[:, None], w_glu, s5_b_glu[:, None],
                     w_out, norm_gains, w_ffn_in, w_ffn_out, w_in)
        if l < DEPTH - 1:
            x_all, u, xr, gr = outs
            x_src = (x_all,)
    return outs[0]
```

```python
import functools
import math

import jax
import jax.numpy as jnp
from jax import lax
from jax.experimental import pallas as pl
from jax.experimental.pallas import tpu as pltpu

F32 = jnp.float32
BF16 = jnp.bfloat16

D_MODEL = 1024
BATCH = 8
SEQ = 4096
DEPTH = 4
GRID_W = 64
GRID_H = SEQ // GRID_W
CTX_LEN = 256
D_S5 = 512
S5_GROUP = 16
S5_GROUPS = 32
S5_STATE = 64
D_LRU = 512
LRU_HEADS = 8
LRU_HEAD_DIM = 64
CONV_W = 4
CONV_LEFT = 2
LRU_C = 8.0
D_IN = D_S5 + 2 * D_LRU
D_FF = 2816
EPS = 1e-6

LANES = 128
SUBLANES = 8
TB = 64
TM = TB * BATCH
N_LAT_BLK = SEQ // TB
N_CTX_BLK = CTX_LEN // TB
N_BLK = N_LAT_BLK + N_CTX_BLK
ROWS_LAT = SEQ * BATCH
ROWS_ALL = (SEQ + CTX_LEN) * BATCH
N_SLAB = D_S5 // LANES
GROUPS_PER_SLAB = LANES // S5_GROUP
SLAB_STATE = GROUPS_PER_SLAB * S5_STATE
S5_TILES = SLAB_STATE // LANES
S5_TB = 128
S5_TM = S5_TB * BATCH
S5_N_LAT = SEQ // S5_TB
S5_N_CTX = CTX_LEN // S5_TB
S5_N_BLK = S5_N_LAT + S5_N_CTX
LRU_HALF = 256
LRU_RUN = GRID_H
assert TB == LRU_RUN
FF_CHUNKS = ((0, 1024), (1024, 1024), (2048, 768))
TAIL_SPLIT = 2
TAIL_ROWS = TM // TAIL_SPLIT
VMEM_LIMIT = 58 * 1024 * 1024


def _params(sem):
    return pltpu.CompilerParams(dimension_semantics=sem, vmem_limit_bytes=VMEM_LIMIT)


def _rms(x, g):
    var = jnp.mean(x * x, axis=-1, keepdims=True)
    return x * lax.rsqrt(var + EPS) * g


def _per_batch(x, fn):
    r, ch = x.shape
    return fn(x.reshape(r // BATCH, BATCH, ch)).reshape(r, ch)


def _gelu(x):
    return jax.nn.gelu(x, approximate=True)


def _sigmoid(x):
    return 0.5 * jnp.tanh(0.5 * x) + 0.5


def _adaln_kernel(c_ref, w_ref, b_ref, o_ref):
    c = c_ref[...]
    s = (c * jax.nn.sigmoid(c)).astype(BF16)
    o_ref[0] = jnp.dot(s, w_ref[0].astype(BF16), preferred_element_type=F32) + b_ref[0]


def _adaln(cond, w_ada, b_ada):
    rows = cond.shape[0]
    nb = 6
    return pl.pallas_call(
        _adaln_kernel,
        grid=(DEPTH, nb),
        in_specs=[
            pl.BlockSpec((rows, D_MODEL), lambda l, n: (0, 0)),
            pl.BlockSpec((1, D_MODEL, D_MODEL), lambda l, n: (l, 0, n)),
            pl.BlockSpec((1, 1, D_MODEL), lambda l, n: (l, 0, n)),
        ],
        out_specs=pl.BlockSpec((1, rows, D_MODEL), lambda l, n: (l, 0, n)),
        out_shape=jax.ShapeDtypeStruct((DEPTH, rows, 6 * D_MODEL), F32),
        compiler_params=_params(("arbitrary", "arbitrary")),
        name="adaln",
    )(cond, w_ada, b_ada.reshape(DEPTH, 1, 6 * D_MODEL))


def _mod_index(i):
    return (i >= N_LAT_BLK).astype(jnp.int32)


def _layer_spec(l, shape, **kw):
    zeros = (0,) * len(shape)
    return pl.BlockSpec((None,) + tuple(shape), lambda i: (l,) + zeros, **kw)


def _mod_spec(l):
    return pl.BlockSpec((None, 1, BATCH, 6 * D_MODEL), lambda i: (l, _mod_index(i), 0, 0))


def _premix_math(x, mod, g, w_ref, u_ref, xr_ref, gr_ref):
    shift, scale = mod[:, 0:D_MODEL], mod[:, D_MODEL:2 * D_MODEL]
    h = _per_batch(_rms(x, g), lambda t: t * (1.0 + scale)[None] + shift[None])
    p = jnp.dot(h.astype(BF16), w_ref[...], preferred_element_type=F32)
    u_ref[...] = p[:, 0:D_S5].astype(u_ref.dtype)
    xr_ref[...] = p[:, D_S5:D_S5 + D_LRU]
    gr_ref[...] = p[:, D_S5 + D_LRU:].astype(gr_ref.dtype)


def _input_rows(x_ref, ctx_ref):
    blk = jnp.where(pl.program_id(0) < N_LAT_BLK, x_ref[...], ctx_ref[...])
    return jnp.swapaxes(blk, 0, 1).reshape(TM, D_MODEL)


_X_SPEC = pl.BlockSpec((BATCH, TB, D_MODEL), lambda i: (0, jnp.minimum(i, N_LAT_BLK - 1), 0))
_CTX_SPEC = pl.BlockSpec((BATCH, TB, D_MODEL), lambda i: (0, jnp.maximum(i - N_LAT_BLK, 0), 0))


def _premix_kernel(x_ref, ctx_ref, mod_ref, g_ref, w_ref, u_ref, xr_ref, gr_ref):
    _premix_math(_input_rows(x_ref, ctx_ref), mod_ref[0], g_ref[0:1, :], w_ref, u_ref, xr_ref, gr_ref)


def _premix(l, x, ctx, mod, gains, w_in):
    row = lambda i: (i, 0)
    half = lambda dtype: jax.ShapeDtypeStruct((ROWS_ALL, D_S5), dtype)
    return pl.pallas_call(
        _premix_kernel,
        grid=(N_BLK,),
        in_specs=[
            _X_SPEC,
            _CTX_SPEC,
            _mod_spec(l),
            _layer_spec(l, (4, D_MODEL)),
            _layer_spec(l, (D_MODEL, D_IN)),
        ],
        out_specs=[pl.BlockSpec((TM, D_S5), row)] * 3,
        out_shape=[half(BF16), half(F32), half(BF16)],
        compiler_params=_params(("parallel",)),
        name="premix",
    )(x, ctx, mod, gains, w_in)


def _s5_kernel(uf_ref, ub_ref, win_ref, wout_ref, wd_ref, a2_ref, yf_ref, yb_ref, st_scr, carry_scr):
    @pl.when(pl.program_id(0) == 0)
    def _():
        st_scr[...] = jnp.zeros_like(st_scr)
        carry_scr[...] = jnp.zeros_like(carry_scr)

    u_refs = (uf_ref, ub_ref)
    y_refs = (yf_ref, yb_ref)
    np_ = S5_TB // 2
    dot = lambda p, q: lax.dot_general(p, q, (((1,), (0,)), ((), ())), preferred_element_type=F32)
    for j in range(N_SLAB):
        lanes = slice(j * LANES, (j + 1) * LANES)
        for d in range(2):
            u = u_refs[d][:, lanes].astype(F32).reshape(np_, 2, BATCH, LANES)
            even, odd = u[:, 0].reshape(np_ * BATCH, LANES), u[:, 1].reshape(np_ * BATCH, LANES)
            lhs = jnp.concatenate([even, odd] if d == 0 else [odd, even], axis=1)
            out = None
            for m in range(S5_TILES):
                z = dot(lhs, win_ref[d, j, m])
                ar, ai = a2_ref[d, j, m, 0], a2_ref[d, j, m, 1]
                sr, si = st_scr[d, j, m, 0], st_scr[d, j, m, 1]
                states = [None] * np_
                for p in range(np_):
                    r = np_ - 1 - p if d == 1 else p
                    zr = z[r * BATCH:(r + 1) * BATCH]
                    sr, si = (ar * sr - ai * si + zr[:, 0:LANES],
                              ar * si + ai * sr + zr[:, LANES:])
                    states[r] = jnp.concatenate([sr, si], axis=1)
                st_scr[d, j, m, 0] = sr
                st_scr[d, j, m, 1] = si
                part = dot(jnp.concatenate(states, axis=0), wout_ref[d, j, m])
                out = part if out is None else out + part
            direct, ahead = out[:, 0:LANES], out[:, LANES:]
            carry = carry_scr[d, j]
            if d == 0:
                carry_scr[d, j] = ahead[(np_ - 1) * BATCH:]
                y_even = jnp.concatenate([carry, ahead[:(np_ - 1) * BATCH]], axis=0) + dot(even, wd_ref[d, j])
                y_odd = direct
            else:
                carry_scr[d, j] = ahead[:BATCH]
                y_odd = jnp.concatenate([ahead[BATCH:], carry], axis=0) + dot(odd, wd_ref[d, j])
                y_even = direct
            y = jnp.stack([y_even.reshape(np_, BATCH, LANES), y_odd.reshape(np_, BATCH, LANES)], axis=1)
            y_refs[d][:, lanes] = y.reshape(S5_TM, LANES).astype(y_refs[d].dtype)


def _s5_fwd_block(i):
    return jnp.where(i < S5_N_CTX, S5_N_LAT + i, i - S5_N_CTX)


def _s5_bwd_block(i):
    return S5_N_BLK - 1 - i


def _s5(l, u, w_in, w_out, w_d, a2):
    out = jax.ShapeDtypeStruct((ROWS_ALL, D_S5), BF16)
    return pl.pallas_call(
        _s5_kernel,
        grid=(S5_N_BLK,),
        in_specs=[
            pl.BlockSpec((S5_TM, D_S5), lambda i: (_s5_fwd_block(i), 0)),
            pl.BlockSpec((S5_TM, D_S5), lambda i: (_s5_bwd_block(i), 0)),
            _layer_spec(l, (2, N_SLAB, S5_TILES, 2 * LANES, 2 * LANES)),
            _layer_spec(l, (2, N_SLAB, S5_TILES, 2 * LANES, 2 * LANES)),
            _layer_spec(l, (2, N_SLAB, LANES, LANES)),
            _layer_spec(l, (2, N_SLAB, S5_TILES, 2, BATCH, LANES)),
        ],
        out_specs=[
            pl.BlockSpec((S5_TM, D_S5), lambda i: (_s5_fwd_block(i), 0)),
            pl.BlockSpec((S5_TM, D_S5), lambda i: (_s5_bwd_block(i), 0)),
        ],
        out_shape=[out, out],
        scratch_shapes=[pltpu.VMEM((2, N_SLAB, S5_TILES, 2, BATCH, LANES), F32),
                        pltpu.VMEM((2, N_SLAB, BATCH, LANES), F32)],
        compiler_params=_params(("arbitrary",)),
        name="s5_scan",
    )(u, u, w_in, w_out, w_d, a2)


def _lru_conv(xe, n, cw_ref, cb_ref):
    out = xe[0:n] * cw_ref[0:1, :][None]
    for k in range(1, CONV_W):
        out = out + xe[k:k + n] * cw_ref[k:k + 1, :][None]
    return out + cb_ref[...][None]


def _lru_coeffs(xc, d, wg_ref, bg_ref, lam_ref, a_scr, b_scr):
    n = xc.shape[0]
    x2 = xc.reshape(n * BATCH, D_LRU)
    for hb in range(D_LRU // LRU_HALF):
        sl = slice(hb * LRU_HALF, (hb + 1) * LRU_HALF)
        xh = x2[:, sl]
        g = jnp.dot(xh.astype(BF16), wg_ref[d, hb], preferred_element_type=F32)
        tr = jnp.tanh(g[:, 0:LRU_HALF] + bg_ref[d, 0:1, sl])
        ti = jnp.tanh(g[:, LRU_HALF:] + bg_ref[d, 1:2, sl])
        lam = lam_ref[d:d + 1, sl]
        softplus = jnp.maximum(-lam, 0.0) + jnp.log1p(jnp.exp(-jnp.abs(lam)))
        k = (-0.5 * LRU_C * math.log2(math.e)) * softplus
        a = jnp.exp2(k * tr + k)
        rest = 1.0 - a * a
        mult = jnp.where(rest > 0.0, rest * lax.rsqrt(rest), 0.0)
        b = mult * (ti * xh + xh)
        a_scr[d, :, :, sl] = a.reshape(n, BATCH, LRU_HALF)
        b_scr[d, :, :, sl] = b.reshape(n, BATCH, LRU_HALF)


def _lru_scan(n, hf0, hb0, a_scr, b_scr, write_f, write_b):
    def body(r, carry):
        hf, hb = carry
        rb = n - 1 - r
        hf = a_scr[0, r] * hf + b_scr[0, r]
        hb = a_scr[1, rb] * hb + b_scr[1, rb]
        write_f(r, hf)
        write_b(rb, hb)
        return hf, hb
    return lax.fori_loop(0, n, body, (hf0, hb0), unroll=4)


def _lru_kernel(n_runs, n_aliased, xf_ref, xfp_ref, xfn_ref, xb_ref, xbp_ref, xbn_ref,
                cw_ref, cb_ref, wg_ref, bg_ref, lam_ref, st0_ref, *refs):
    hf_ref, hb_ref, st_ref, a_scr, b_scr = refs[n_aliased:]
    i = pl.program_id(0)
    n = LRU_RUN

    @pl.when(i == 0)
    def _():
        st_ref[...] = st0_ref[...]

    def run(x_ref, prev_ref, next_ref, w):
        has_prev = (w > 0).astype(F32)
        has_next = (w < n_runs - 1).astype(F32)
        xe = jnp.concatenate(
            [prev_ref[...] * has_prev, x_ref[...], next_ref[...] * has_next], axis=0)
        return _lru_conv(xe, n, cw_ref, cb_ref)

    _lru_coeffs(run(xf_ref, xfp_ref, xfn_ref, i), 0, wg_ref, bg_ref, lam_ref, a_scr, b_scr)
    _lru_coeffs(run(xb_ref, xbp_ref, xbn_ref, n_runs - 1 - i), 1, wg_ref, bg_ref, lam_ref,
                a_scr, b_scr)

    def write_f(r, h):
        hf_ref[r] = h

    def write_b(r, h):
        hb_ref[r] = h

    hf, hb = _lru_scan(n, st_ref[0], st_ref[1], a_scr, b_scr, write_f, write_b)
    st_ref[0] = hf
    st_ref[1] = hb


def _lru_call(name, l, n_runs, run_spec, prev_spec, next_spec, x4, weights, st0, hf_in, hb_in):
    full = jax.ShapeDtypeStruct((N_BLK, TB, BATCH, D_LRU), F32)
    st_spec = pl.BlockSpec((2, BATCH, D_LRU), lambda i: (0, 0, 0))
    wspecs = [
        _layer_spec(l, (CONV_W, D_LRU)),
        _layer_spec(l, (1, D_LRU)),
        _layer_spec(l, (2, 2, LRU_HALF, 2 * LRU_HALF)),
        _layer_spec(l, (2, 2, D_LRU)),
        _layer_spec(l, (2, D_LRU)),
    ]
    fw = lambda i: i
    bw = lambda i: n_runs - 1 - i
    aliased = [] if hf_in is None else [hf_in, hb_in]
    n_in = 6 + len(wspecs) + 1
    return pl.pallas_call(
        functools.partial(_lru_kernel, n_runs, len(aliased)),
        grid=(n_runs,),
        in_specs=[run_spec(fw), prev_spec(fw), next_spec(fw),
                  run_spec(bw), prev_spec(bw), next_spec(bw)] + wspecs
        + [st_spec] + [pl.BlockSpec(memory_space=pl.ANY)] * len(aliased),
        out_specs=[run_spec(fw), run_spec(bw), st_spec],
        out_shape=[full, full, jax.ShapeDtypeStruct((2, BATCH, D_LRU), F32)],
        scratch_shapes=[pltpu.VMEM((2, LRU_RUN, BATCH, D_LRU), F32)] * 2,
        input_output_aliases={n_in + k: k for k in range(len(aliased))},
        compiler_params=_params(("arbitrary",)),
        name=name,
    )(x4, x4, x4, x4, x4, x4, *weights, st0, *aliased)


def _lru(l, xr, weights):
    x4 = xr.reshape(N_BLK, TB, BATCH, D_LRU)
    st0 = jnp.zeros((2, BATCH, D_LRU), F32)

    c0 = N_LAT_BLK
    clip = lambda k, n: jnp.clip(k, 0, n - 1)
    ctx_run = lambda f: pl.BlockSpec((None, TB, BATCH, D_LRU), lambda i: (c0 + f(i), 0, 0, 0))
    ctx_prev = lambda f: pl.BlockSpec(
        (None, CONV_LEFT, BATCH, D_LRU),
        lambda i: (c0 + clip(f(i) - 1, N_CTX_BLK), TB // CONV_LEFT - 1, 0, 0))
    ctx_next = lambda f: pl.BlockSpec(
        (None, 1, BATCH, D_LRU), lambda i: (c0 + clip(f(i) + 1, N_CTX_BLK), 0, 0, 0))
    hf, hb, st = _lru_call("lru_ctx", l, N_CTX_BLK, ctx_run, ctx_prev, ctx_next,
                           x4, weights, st0, None, None)

    lat_run = lambda f: pl.BlockSpec((GRID_H, None, BATCH, D_LRU), lambda i: (0, f(i), 0, 0))
    lat_prev = lambda f: pl.BlockSpec(
        (CONV_LEFT, None, BATCH, D_LRU),
        lambda i: (GRID_H // CONV_LEFT - 1, clip(f(i) - 1, GRID_W), 0, 0))
    lat_next = lambda f: pl.BlockSpec(
        (1, None, BATCH, D_LRU), lambda i: (0, clip(f(i) + 1, GRID_W), 0, 0))
    hf, hb, _ = _lru_call("lru_lat", l, GRID_W, lat_run, lat_prev, lat_next,
                          x4, weights, st, hf, hb)
    return hf.reshape(ROWS_ALL, D_LRU), hb.reshape(ROWS_ALL, D_LRU)


def _tail_kernel(first, has_next, *refs):
    if first:
        x_in = _input_rows(refs[0], refs[1])
        load_x = lambda rows: x_in[rows]
        refs = refs[2:]
    else:
        x_ref = refs[0]
        load_x = lambda rows: x_ref[rows, :]
        refs = refs[1:]
    (u_ref, yf_ref, yb_ref, gr_ref, hf_ref, hb_ref, mod_ref,
     d_ref, wglu_ref, bglu_ref, wout_ref, gains_ref, wfin_ref, wfout_ref) = refs[:14]
    refs = refs[14:]
    mod = mod_ref[0]
    chunk = lambda k: mod[:, k * D_MODEL:(k + 1) * D_MODEL]
    gate1, shift2, scale2, gate2 = chunk(2), chunk(3), chunk(4), chunk(5)
    groups = [slice(s * TAIL_ROWS, (s + 1) * TAIL_ROWS) for s in range(TAIL_SPLIT)]

    def mixer_out(rows):
        f32 = lambda ref: ref[rows, :].astype(F32)
        ys = _gelu(d_ref[...] * f32(u_ref) + f32(yf_ref) + f32(yb_ref))
        z = jnp.dot(ys.astype(BF16), wglu_ref[...], preferred_element_type=F32) + bglu_ref[...]
        y_s5 = ys * _sigmoid(z)
        y_lru = (f32(hf_ref) + f32(hb_ref)) * _gelu(f32(gr_ref))
        out = jnp.dot(y_s5.astype(BF16), wout_ref[0:D_S5, :], preferred_element_type=F32)
        out = out + jnp.dot(y_lru.astype(BF16), wout_ref[D_S5:, :], preferred_element_type=F32)
        return load_x(rows) + _per_batch(_rms(out, gains_ref[1:2, :]), lambda t: t * gate1[None])

    def ffn(x):
        h = _per_batch(_rms(x, gains_ref[2:3, :]), lambda t: t * (1.0 + scale2)[None] + shift2[None])
        h = h.astype(BF16)
        f = jnp.zeros((TAIL_ROWS, D_MODEL), F32)
        for start, size in FF_CHUNKS:
            gt = jnp.dot(h, wfin_ref[:, start:start + size], preferred_element_type=F32)
            up = jnp.dot(h, wfin_ref[:, D_FF + start:D_FF + start + size], preferred_element_type=F32)
            act = (gt * _sigmoid(gt) * up).astype(BF16)
            f = f + jnp.dot(act, wfout_ref[start:start + size, :], preferred_element_type=F32)
        return x + _per_batch(_rms(f, gains_ref[3:4, :]), lambda t: t * gate2[None])

    xs = [mixer_out(rows) for rows in groups]
    xs = [ffn(x) for x in xs]
    for s, (rows, x) in enumerate(zip(groups, xs)):
        if has_next:
            modn_ref, gn_ref, win_ref, o_ref, un_ref, xrn_ref, grn_ref = refs
            o_ref[rows, :] = x
            _premix_math(x, modn_ref[0], gn_ref[0:1, :], win_ref,
                         un_ref.at[rows, :], xrn_ref.at[rows, :], grn_ref.at[rows, :])
        else:
            (o_ref,) = refs
            steps = slice(s * TAIL_ROWS // BATCH, (s + 1) * TAIL_ROWS // BATCH)
            o_ref[:, steps, :] = jnp.swapaxes(x.reshape(TAIL_ROWS // BATCH, BATCH, D_MODEL), 0, 1)


def _tail(l, x_src, u, yf, yb, gr, hf, hb, mod, d, wglu, bglu, wout, gains, wfin, wfout, w_in):
    first = len(x_src) == 2
    has_next = l < DEPTH - 1
    n_blk = N_BLK if has_next else N_LAT_BLK
    rows = n_blk * TM
    row = lambda i: (i, 0)
    once = pl.Buffered(1)
    wide = pl.BlockSpec((TM, D_MODEL), row)
    half = pl.BlockSpec((TM, D_S5), row)
    in_specs = ([_X_SPEC, _CTX_SPEC] if first else [wide]) + [half] * 6 + [
        _mod_spec(l),
        _layer_spec(l, (1, D_S5)),
        _layer_spec(l, (D_S5, D_S5), pipeline_mode=once),
        _layer_spec(l, (1, D_S5)),
        _layer_spec(l, (D_MODEL, D_MODEL), pipeline_mode=once),
        _layer_spec(l, (4, D_MODEL)),
        _layer_spec(l, (D_MODEL, 2 * D_FF), pipeline_mode=once),
        _layer_spec(l, (D_FF, D_MODEL), pipeline_mode=once),
    ]
    args = [*x_src, u, yf, yb, gr, hf, hb, mod, d, wglu, bglu, wout, gains, wfin, wfout]
    if has_next:
        in_specs += [_mod_spec(l + 1), _layer_spec(l + 1, (4, D_MODEL)),
                     _layer_spec(l + 1, (D_MODEL, D_IN), pipeline_mode=once)]
        args += [mod, gains, w_in]
        out_specs = [wide] + [half] * 3
        out_shape = [jax.ShapeDtypeStruct((rows, D_MODEL), F32)]
        out_shape += [jax.ShapeDtypeStruct((rows, D_S5), dt) for dt in (BF16, F32, BF16)]
    else:
        out_specs = [pl.BlockSpec((BATCH, TB, D_MODEL), lambda i: (0, i, 0))]
        out_shape = [jax.ShapeDtypeStruct((BATCH, SEQ, D_MODEL), F32)]
    return pl.pallas_call(
        functools.partial(_tail_kernel, first, has_next),
        grid=(n_blk,),
        in_specs=in_specs,
        out_specs=out_specs,
        out_shape=out_shape,
        compiler_params=_params(("parallel",)),
        name="tail",
    )(*args)


def _s5_matrices(a_re, a_im, log_dt, b_re, b_im, c_re, c_im):
    dt = jnp.exp(log_dt)[..., None]
    mag = jnp.exp(a_re * dt)
    ab_re = mag * jnp.cos(a_im * dt)
    ab_im = mag * jnp.sin(a_im * dt)
    den = a_re * a_re + a_im * a_im
    nr = ab_re - 1.0
    f_re = (nr * a_re + ab_im * a_im) / den
    f_im = (ab_im * a_re - nr * a_im) / den
    bb_re = f_re[..., None] * b_re - f_im[..., None] * b_im
    bb_im = f_re[..., None] * b_im + f_im[..., None] * b_re

    tile_groups = GROUPS_PER_SLAB // S5_TILES
    split = lambda m: m.reshape((DEPTH, 2, N_SLAB, S5_TILES, tile_groups) + m.shape[3:])
    in_tile = (jnp.arange(GROUPS_PER_SLAB)[None, None, :]
               == (tile_groups * jnp.arange(S5_TILES)[:, None, None] + jnp.arange(tile_groups)[None, :, None])
               ).astype(F32)

    ar, ai = ab_re[..., None], ab_im[..., None]
    first = jnp.stack([bb_re * ar - bb_im * ai, bb_re * ai + bb_im * ar], axis=3)
    second = jnp.stack([bb_re, bb_im], axis=3)
    w_in = jnp.einsum('ldjmhxcnp,mhg->ldjmxgpchn', split(jnp.stack([first, second], axis=3)), in_tile)
    w_in = w_in.reshape(DEPTH, 2, N_SLAB, S5_TILES, 2 * LANES, 2 * LANES).astype(BF16)

    ar, ai = ab_re[..., None, :], ab_im[..., None, :]
    now = jnp.stack([c_re, -c_im], axis=3)
    ahead = jnp.stack([c_re * ar - c_im * ai, -(c_re * ai + c_im * ar)], axis=3)
    w_out = jnp.einsum('ldjmhwcpn,mhg->ldjmchnwgp', split(jnp.stack([now, ahead], axis=3)), in_tile)
    w_out = w_out.reshape(DEPTH, 2, N_SLAB, S5_TILES, 2 * LANES, 2 * LANES).astype(BF16)

    hi = lax.Precision.HIGHEST
    d_group = (jnp.einsum('ldgnp,ldgqn->ldgpq', bb_re, c_re, precision=hi)
               - jnp.einsum('ldgnp,ldgqn->ldgpq', bb_im, c_im, precision=hi))
    d_group = d_group.reshape(DEPTH, 2, N_SLAB, GROUPS_PER_SLAB, S5_GROUP, S5_GROUP)
    w_d = jnp.einsum('ldjgpq,gh->ldjgphq', d_group, jnp.eye(GROUPS_PER_SLAB, dtype=F32))
    w_d = w_d.reshape(DEPTH, 2, N_SLAB, LANES, LANES).astype(BF16)

    tiles = lambda m: m.reshape(DEPTH, 2, N_SLAB, S5_TILES, LANES)
    a2 = jnp.stack([tiles(ab_re * ab_re - ab_im * ab_im), tiles(2.0 * ab_re * ab_im)], axis=4)
    a2 = jnp.broadcast_to(a2[..., None, :], (DEPTH, 2, N_SLAB, S5_TILES, 2, BATCH, LANES))
    return w_in, w_out, w_d, a2


def _lru_gate_matrix(w_rg, w_ig):
    heads = LRU_HALF // LRU_HEAD_DIM
    halves = D_LRU // LRU_HALF
    eye = jnp.eye(heads, dtype=F32)

    def bd(w):
        w = w.reshape(DEPTH, 2, halves, heads, LRU_HEAD_DIM, LRU_HEAD_DIM)
        return jnp.einsum('ldbhij,hk->ldbhikj', w, eye).reshape(DEPTH, 2, halves, LRU_HALF, LRU_HALF)

    return jnp.concatenate([bd(w_rg), bd(w_ig)], axis=-1).astype(BF16)


def kernel(x, c, ctx, c_ctx, w_ada, b_ada, norm_gains, w_in, s5_a_re, s5_a_im, s5_log_dt, s5_b_re, s5_b_im, s5_c_re, s5_c_im, s5_d, s5_w_glu, s5_b_glu, lru_conv_w, lru_conv_b, lru_w_rg, lru_b_rg, lru_w_ig, lru_b_ig, lru_lambda, w_out, w_ffn_in, w_ffn_out):
    assert x.shape == (BATCH, SEQ, D_MODEL) and ctx.shape == (BATCH, CTX_LEN, D_MODEL)
    cond = jnp.concatenate([c, c_ctx[None], jnp.zeros((2 * BATCH - BATCH - 1, D_MODEL), F32)], axis=0)
    m = _adaln(cond, w_ada, b_ada)
    mod = jnp.stack(
        [m[:, :BATCH], jnp.broadcast_to(m[:, BATCH:BATCH + 1], (DEPTH, BATCH, 6 * D_MODEL))], axis=1)

    w_in, w_glu, w_out, w_ffn_in, w_ffn_out = (
        w.astype(BF16) for w in (w_in, s5_w_glu, w_out, w_ffn_in, w_ffn_out))
    s5_w = _s5_matrices(s5_a_re, s5_a_im, s5_log_dt, s5_b_re, s5_b_im, s5_c_re, s5_c_im)
    lru_w = (0.5 * lru_conv_w, 0.5 * lru_conv_b[:, None], _lru_gate_matrix(lru_w_rg, lru_w_ig),
             0.5 * jnp.stack([lru_b_rg, lru_b_ig], axis=2), lru_lambda)

    u, xr, gr = _premix(0, x, ctx, mod, norm_gains, w_in)
    x_src = (x, ctx)
    for l in range(DEPTH):
        yf, yb = _s5(l, u, *s5_w)
        hf, hb = _lru(l, xr, lru_w)
        outs = _tail(l, x_src, u, yf, yb, gr, hf, hb, mod, s5_d[:, None], w_glu, s5_b_glu[:, None],
                     w_out, norm_gains, w_ffn_in, w_ffn_out, w_in)
        if l < DEPTH - 1:
            x_all, u, xr, gr = outs
            x_src = (x_all,)
    return outs[0]
```

```python
import functools
import math

import jax
import jax.numpy as jnp
from jax import lax
from jax.experimental import pallas as pl
from jax.experimental.pallas import tpu as pltpu

F32 = jnp.float32
BF16 = jnp.bfloat16

D_MODEL = 1024
BATCH = 8
SEQ = 4096
DEPTH = 4
GRID_W = 64
GRID_H = SEQ // GRID_W
CTX_LEN = 256
D_S5 = 512
S5_GROUP = 16
S5_GROUPS = 32
S5_STATE = 64
D_LRU = 512
LRU_HEADS = 8
LRU_HEAD_DIM = 64
CONV_W = 4
CONV_LEFT = 2
LRU_C = 8.0
D_IN = D_S5 + 2 * D_LRU
D_FF = 2816
EPS = 1e-6

LANES = 128
SUBLANES = 8
TB = 64
TM = TB * BATCH
N_LAT_BLK = SEQ // TB
N_CTX_BLK = CTX_LEN // TB
N_BLK = N_LAT_BLK + N_CTX_BLK
ROWS_LAT = SEQ * BATCH
ROWS_ALL = (SEQ + CTX_LEN) * BATCH
N_SLAB = D_S5 // LANES
GROUPS_PER_SLAB = LANES // S5_GROUP
SLAB_STATE = GROUPS_PER_SLAB * S5_STATE
S5_TILES = SLAB_STATE // LANES
S5_TB = 128
S5_TM = S5_TB * BATCH
S5_N_LAT = SEQ // S5_TB
S5_N_CTX = CTX_LEN // S5_TB
S5_N_BLK = S5_N_LAT + S5_N_CTX
LRU_HALF = 256
LRU_RUN = GRID_H
assert TB == LRU_RUN
FF_CHUNKS = ((0, 1024), (1024, 1024), (2048, 768))
TAIL_SPLIT = 2
TAIL_ROWS = TM // TAIL_SPLIT
VMEM_LIMIT = 58 * 1024 * 1024


def _params(sem):
    return pltpu.CompilerParams(dimension_semantics=sem, vmem_limit_bytes=VMEM_LIMIT)


def _rms(x, g):
    var = jnp.mean(x * x, axis=-1, keepdims=True)
    return x * lax.rsqrt(var + EPS) * g


def _per_batch(x, fn):
    r, ch = x.shape
    return fn(x.reshape(r // BATCH, BATCH, ch)).reshape(r, ch)


def _gelu(x):
    return jax.nn.gelu(x, approximate=True)


def _sigmoid(x):
    return 0.5 * jnp.tanh(0.5 * x) + 0.5


def _adaln_kernel(c_ref, w_ref, b_ref, o_ref):
    c = c_ref[...]
    s = (c * jax.nn.sigmoid(c)).astype(BF16)
    o_ref[0] = jnp.dot(s, w_ref[0].astype(BF16), preferred_element_type=F32) + b_ref[0]


def _adaln(cond, w_ada, b_ada):
    rows = cond.shape[0]
    nb = 6
    return pl.pallas_call(
        _adaln_kernel,
        grid=(DEPTH, nb),
        in_specs=[
            pl.BlockSpec((rows, D_MODEL), lambda l, n: (0, 0)),
            pl.BlockSpec((1, D_MODEL, D_MODEL), lambda l, n: (l, 0, n)),
            pl.BlockSpec((1, 1, D_MODEL), lambda l, n: (l, 0, n)),
        ],
        out_specs=pl.BlockSpec((1, rows, D_MODEL), lambda l, n: (l, 0, n)),
        out_shape=jax.ShapeDtypeStruct((DEPTH, rows, 6 * D_MODEL), F32),
        compiler_params=_params(("arbitrary", "arbitrary")),
        name="adaln",
    )(cond, w_ada, b_ada.reshape(DEPTH, 1, 6 * D_MODEL))


def _mod_index(i):
    return (i >= N_LAT_BLK).astype(jnp.int32)


def _layer_spec(l, shape, **kw):
    zeros = (0,) * len(shape)
    return pl.BlockSpec((None,) + tuple(shape), lambda i: (l,) + zeros, **kw)


def _mod_spec(l):
    return pl.BlockSpec((None, 1, BATCH, 6 * D_MODEL), lambda i: (l, _mod_index(i), 0, 0))


def _premix_math(x, mod, g, w_ref, u_ref, xr_ref, gr_ref):
    shift, scale = mod[:, 0:D_MODEL], mod[:, D_MODEL:2 * D_MODEL]
    h = _per_batch(_rms(x, g), lambda t: t * (1.0 + scale)[None] + shift[None])
    p = jnp.dot(h.astype(BF16), w_ref[...], preferred_element_type=F32)
    u_ref[...] = p[:, 0:D_S5].astype(u_ref.dtype)
    xr_ref[...] = p[:, D_S5:D_S5 + D_LRU]
    gr_ref[...] = p[:, D_S5 + D_LRU:].astype(gr_ref.dtype)


def _input_rows(x_ref, ctx_ref):
    blk = jnp.where(pl.program_id(0) < N_LAT_BLK, x_ref[...], ctx_ref[...])
    return jnp.swapaxes(blk, 0, 1).reshape(TM, D_MODEL)


_X_SPEC = pl.BlockSpec((BATCH, TB, D_MODEL), lambda i: (0, jnp.minimum(i, N_LAT_BLK - 1), 0))
_CTX_SPEC = pl.BlockSpec((BATCH, TB, D_MODEL), lambda i: (0, jnp.maximum(i - N_LAT_BLK, 0), 0))


def _premix_kernel(x_ref, ctx_ref, mod_ref, g_ref, w_ref, u_ref, xr_ref, gr_ref):
    _premix_math(_input_rows(x_ref, ctx_ref), mod_ref[0], g_ref[0:1, :], w_ref, u_ref, xr_ref, gr_ref)


def _premix(l, x, ctx, mod, gains, w_in):
    row = lambda i: (i, 0)
    half = lambda dtype: jax.ShapeDtypeStruct((ROWS_ALL, D_S5), dtype)
    return pl.pallas_call(
        _premix_kernel,
        grid=(N_BLK,),
        in_specs=[
            _X_SPEC,
            _CTX_SPEC,
            _mod_spec(l),
            _layer_spec(l, (4, D_MODEL)),
            _layer_spec(l, (D_MODEL, D_IN)),
        ],
        out_specs=[pl.BlockSpec((TM, D_S5), row)] * 3,
        out_shape=[half(BF16), half(F32), half(BF16)],
        compiler_params=_params(("parallel",)),
        name="premix",
    )(x, ctx, mod, gains, w_in)


def _s5_expand(compact):
    rows, cols = 2 * LANES, 2 * SLAB_STATE
    full = jnp.broadcast_to(compact.reshape(2, 1, S5_GROUP, cols), (2, GROUPS_PER_SLAB, S5_GROUP, cols))
    row_group = (lax.broadcasted_iota(jnp.int32, (rows, cols), 0) // S5_GROUP) % GROUPS_PER_SLAB
    col_group = (lax.broadcasted_iota(jnp.int32, (rows, cols), 1) // S5_STATE) % GROUPS_PER_SLAB
    return jnp.where(row_group == col_group, full.reshape(rows, cols), 0.0)


def _s5_kernel(uf_ref, ub_ref, cin_ref, cout_ref, wd_ref, a2_ref, yf_ref, yb_ref,
               win_scr, wout_scr, st_scr, carry_scr):
    @pl.when(pl.program_id(0) == 0)
    def _():
        st_scr[...] = jnp.zeros_like(st_scr)
        carry_scr[...] = jnp.zeros_like(carry_scr)
        for d in range(2):
            for j in range(N_SLAB):
                win_scr[d, j] = _s5_expand(cin_ref[d, j]).astype(BF16)
                wout_scr[d, j] = _s5_expand(cout_ref[d, j]).T.astype(BF16)

    def tile(m, piece):
        return [piece(slice(m * LANES, (m + 1) * LANES)),
                piece(slice(SLAB_STATE + m * LANES, SLAB_STATE + (m + 1) * LANES))]

    u_refs = (uf_ref, ub_ref)
    y_refs = (yf_ref, yb_ref)
    np_ = S5_TB // 2
    dot = lambda p, q: lax.dot_general(p, q, (((1,), (0,)), ((), ())), preferred_element_type=F32)
    for j in range(N_SLAB):
        lanes = slice(j * LANES, (j + 1) * LANES)
        for d in range(2):
            u = u_refs[d][:, lanes].astype(F32).reshape(np_, 2, BATCH, LANES)
            even, odd = u[:, 0].reshape(np_ * BATCH, LANES), u[:, 1].reshape(np_ * BATCH, LANES)
            lhs = jnp.concatenate([even, odd] if d == 0 else [odd, even], axis=1)
            out = None
            for m in range(S5_TILES):
                w_in = jnp.concatenate(tile(m, lambda c: win_scr[d, j, :, c]), axis=1)
                w_out = jnp.concatenate(tile(m, lambda c: wout_scr[d, j, c, :]), axis=0)
                z = dot(lhs, w_in)
                ar, ai = a2_ref[d, j, m, 0], a2_ref[d, j, m, 1]
                sr, si = st_scr[d, j, m, 0], st_scr[d, j, m, 1]
                states = [None] * np_
                for p in range(np_):
                    r = np_ - 1 - p if d == 1 else p
                    zr = z[r * BATCH:(r + 1) * BATCH]
                    sr, si = (ar * sr - ai * si + zr[:, 0:LANES],
                              ar * si + ai * sr + zr[:, LANES:])
                    states[r] = jnp.concatenate([sr, si], axis=1)
                st_scr[d, j, m, 0] = sr
                st_scr[d, j, m, 1] = si
                part = dot(jnp.concatenate(states, axis=0), w_out)
                out = part if out is None else out + part
            direct, ahead = out[:, 0:LANES], out[:, LANES:]
            carry = carry_scr[d, j]
            if d == 0:
                carry_scr[d, j] = ahead[(np_ - 1) * BATCH:]
                y_even = jnp.concatenate([carry, ahead[:(np_ - 1) * BATCH]], axis=0) + dot(even, wd_ref[d, j])
                y_odd = direct
            else:
                carry_scr[d, j] = ahead[:BATCH]
                y_odd = jnp.concatenate([ahead[BATCH:], carry], axis=0) + dot(odd, wd_ref[d, j])
                y_even = direct
            y = jnp.stack([y_even.reshape(np_, BATCH, LANES), y_odd.reshape(np_, BATCH, LANES)], axis=1)
            y_refs[d][:, lanes] = y.reshape(S5_TM, LANES).astype(y_refs[d].dtype)


def _s5_fwd_block(i):
    return jnp.where(i < S5_N_CTX, S5_N_LAT + i, i - S5_N_CTX)


def _s5_bwd_block(i):
    return S5_N_BLK - 1 - i


def _s5(l, u, c_in, c_out, w_d, a2):
    out = jax.ShapeDtypeStruct((ROWS_ALL, D_S5), BF16)
    return pl.pallas_call(
        _s5_kernel,
        grid=(S5_N_BLK,),
        in_specs=[
            pl.BlockSpec((S5_TM, D_S5), lambda i: (_s5_fwd_block(i), 0)),
            pl.BlockSpec((S5_TM, D_S5), lambda i: (_s5_bwd_block(i), 0)),
            _layer_spec(l, (2, N_SLAB, 2 * S5_GROUP, 2 * SLAB_STATE)),
            _layer_spec(l, (2, N_SLAB, 2 * S5_GROUP, 2 * SLAB_STATE)),
            _layer_spec(l, (2, N_SLAB, LANES, LANES)),
            _layer_spec(l, (2, N_SLAB, S5_TILES, 2, BATCH, LANES)),
        ],
        out_specs=[
            pl.BlockSpec((S5_TM, D_S5), lambda i: (_s5_fwd_block(i), 0)),
            pl.BlockSpec((S5_TM, D_S5), lambda i: (_s5_bwd_block(i), 0)),
        ],
        out_shape=[out, out],
        scratch_shapes=[pltpu.VMEM((2, N_SLAB, 2 * LANES, 2 * SLAB_STATE), BF16),
                        pltpu.VMEM((2, N_SLAB, 2 * SLAB_STATE, 2 * LANES), BF16),
                        pltpu.VMEM((2, N_SLAB, S5_TILES, 2, BATCH, LANES), F32),
                        pltpu.VMEM((2, N_SLAB, BATCH, LANES), F32)],
        compiler_params=_params(("arbitrary",)),
        name="s5_scan",
    )(u, u, c_in, c_out, w_d, a2)


def _lru_conv(xe, n, cw_ref, cb_ref):
    out = xe[0:n] * cw_ref[0:1, :][None]
    for k in range(1, CONV_W):
        out = out + xe[k:k + n] * cw_ref[k:k + 1, :][None]
    return out + cb_ref[...][None]


def _lru_coeffs(xc, d, wg_ref, bg_ref, lam_ref, a_scr, b_scr):
    n = xc.shape[0]
    x2 = xc.reshape(n * BATCH, D_LRU)
    for hb in range(D_LRU // LRU_HALF):
        sl = slice(hb * LRU_HALF, (hb + 1) * LRU_HALF)
        xh = x2[:, sl]
        g = jnp.dot(xh.astype(BF16), wg_ref[d, hb], preferred_element_type=F32)
        tr = jnp.tanh(g[:, 0:LRU_HALF] + bg_ref[d, 0:1, sl])
        ti = jnp.tanh(g[:, LRU_HALF:] + bg_ref[d, 1:2, sl])
        lam = lam_ref[d:d + 1, sl]
        softplus = jnp.maximum(-lam, 0.0) + jnp.log1p(jnp.exp(-jnp.abs(lam)))
        k = (-0.5 * LRU_C * math.log2(math.e)) * softplus
        a = jnp.exp2(k * tr + k)
        rest = 1.0 - a * a
        mult = jnp.where(rest > 0.0, rest * lax.rsqrt(rest), 0.0)
        b = mult * (ti * xh + xh)
        a_scr[d, :, :, sl] = a.reshape(n, BATCH, LRU_HALF)
        b_scr[d, :, :, sl] = b.reshape(n, BATCH, LRU_HALF)


def _lru_scan(n, hf0, hb0, a_scr, b_scr, write_f, write_b):
    def body(r, carry):
        hf, hb = carry
        rb = n - 1 - r
        hf = a_scr[0, r] * hf + b_scr[0, r]
        hb = a_scr[1, rb] * hb + b_scr[1, rb]
        write_f(r, hf)
        write_b(rb, hb)
        return hf, hb
    return lax.fori_loop(0, n, body, (hf0, hb0), unroll=4)


def _lru_kernel(n_runs, n_aliased, xf_ref, xfp_ref, xfn_ref, xb_ref, xbp_ref, xbn_ref,
                cw_ref, cb_ref, wg_ref, bg_ref, lam_ref, st0_ref, *refs):
    hf_ref, hb_ref, st_ref, a_scr, b_scr = refs[n_aliased:]
    i = pl.program_id(0)
    n = LRU_RUN

    @pl.when(i == 0)
    def _():
        st_ref[...] = st0_ref[...]

    def run(x_ref, prev_ref, next_ref, w):
        has_prev = (w > 0).astype(F32)
        has_next = (w < n_runs - 1).astype(F32)
        xe = jnp.concatenate(
            [prev_ref[...] * has_prev, x_ref[...], next_ref[...] * has_next], axis=0)
        return _lru_conv(xe, n, cw_ref, cb_ref)

    _lru_coeffs(run(xf_ref, xfp_ref, xfn_ref, i), 0, wg_ref, bg_ref, lam_ref, a_scr, b_scr)
    _lru_coeffs(run(xb_ref, xbp_ref, xbn_ref, n_runs - 1 - i), 1, wg_ref, bg_ref, lam_ref,
                a_scr, b_scr)

    def write_f(r, h):
        hf_ref[r] = h

    def write_b(r, h):
        hb_ref[r] = h

    hf, hb = _lru_scan(n, st_ref[0], st_ref[1], a_scr, b_scr, write_f, write_b)
    st_ref[0] = hf
    st_ref[1] = hb


def _lru_call(name, l, n_runs, run_spec, prev_spec, next_spec, x4, weights, st0, hf_in, hb_in):
    full = jax.ShapeDtypeStruct((N_BLK, TB, BATCH, D_LRU), F32)
    st_spec = pl.BlockSpec((2, BATCH, D_LRU), lambda i: (0, 0, 0))
    wspecs = [
        _layer_spec(l, (CONV_W, D_LRU)),
        _layer_spec(l, (1, D_LRU)),
        _layer_spec(l, (2, 2, LRU_HALF, 2 * LRU_HALF)),
        _layer_spec(l, (2, 2, D_LRU)),
        _layer_spec(l, (2, D_LRU)),
    ]
    fw = lambda i: i
    bw = lambda i: n_runs - 1 - i
    aliased = [] if hf_in is None else [hf_in, hb_in]
    n_in = 6 + len(wspecs) + 1
    return pl.pallas_call(
        functools.partial(_lru_kernel, n_runs, len(aliased)),
        grid=(n_runs,),
        in_specs=[run_spec(fw), prev_spec(fw), next_spec(fw),
                  run_spec(bw), prev_spec(bw), next_spec(bw)] + wspecs
        + [st_spec] + [pl.BlockSpec(memory_space=pl.ANY)] * len(aliased),
        out_specs=[run_spec(fw), run_spec(bw), st_spec],
        out_shape=[full, full, jax.ShapeDtypeStruct((2, BATCH, D_LRU), F32)],
        scratch_shapes=[pltpu.VMEM((2, LRU_RUN, BATCH, D_LRU), F32)] * 2,
        input_output_aliases={n_in + k: k for k in range(len(aliased))},
        compiler_params=_params(("arbitrary",)),
        name=name,
    )(x4, x4, x4, x4, x4, x4, *weights, st0, *aliased)


def _lru(l, xr, weights):
    x4 = xr.reshape(N_BLK, TB, BATCH, D_LRU)
    st0 = jnp.zeros((2, BATCH, D_LRU), F32)

    c0 = N_LAT_BLK
    clip = lambda k, n: jnp.clip(k, 0, n - 1)
    ctx_run = lambda f: pl.BlockSpec((None, TB, BATCH, D_LRU), lambda i: (c0 + f(i), 0, 0, 0))
    ctx_prev = lambda f: pl.BlockSpec(
        (None, CONV_LEFT, BATCH, D_LRU),
        lambda i: (c0 + clip(f(i) - 1, N_CTX_BLK), TB // CONV_LEFT - 1, 0, 0))
    ctx_next = lambda f: pl.BlockSpec(
        (None, 1, BATCH, D_LRU), lambda i: (c0 + clip(f(i) + 1, N_CTX_BLK), 0, 0, 0))
    hf, hb, st = _lru_call("lru_ctx", l, N_CTX_BLK, ctx_run, ctx_prev, ctx_next,
                           x4, weights, st0, None, None)

    lat_run = lambda f: pl.BlockSpec((GRID_H, None, BATCH, D_LRU), lambda i: (0, f(i), 0, 0))
    lat_prev = lambda f: pl.BlockSpec(
        (CONV_LEFT, None, BATCH, D_LRU),
        lambda i: (GRID_H // CONV_LEFT - 1, clip(f(i) - 1, GRID_W), 0, 0))
    lat_next = lambda f: pl.BlockSpec(
        (1, None, BATCH, D_LRU), lambda i: (0, clip(f(i) + 1, GRID_W), 0, 0))
    hf, hb, _ = _lru_call("lru_lat", l, GRID_W, lat_run, lat_prev, lat_next,
                          x4, weights, st, hf, hb)
    return hf.reshape(ROWS_ALL, D_LRU), hb.reshape(ROWS_ALL, D_LRU)


def _tail_kernel(first, has_next, *refs):
    if first:
        x_in = _input_rows(refs[0], refs[1])
        load_x = lambda rows: x_in[rows]
        refs = refs[2:]
    else:
        x_ref = refs[0]
        load_x = lambda rows: x_ref[rows, :]
        refs = refs[1:]
    (u_ref, yf_ref, yb_ref, gr_ref, hf_ref, hb_ref, mod_ref,
     d_ref, wglu_ref, bglu_ref, wout_ref, gains_ref, wfin_ref, wfout_ref) = refs[:14]
    refs = refs[14:]
    mod = mod_ref[0]
    chunk = lambda k: mod[:, k * D_MODEL:(k + 1) * D_MODEL]
    gate1, shift2, scale2, gate2 = chunk(2), chunk(3), chunk(4), chunk(5)
    groups = [slice(s * TAIL_ROWS, (s + 1) * TAIL_ROWS) for s in range(TAIL_SPLIT)]

    def mixer_out(rows):
        f32 = lambda ref: ref[rows, :].astype(F32)
        ys = _gelu(d_ref[...] * f32(u_ref) + f32(yf_ref) + f32(yb_ref))
        z = jnp.dot(ys.astype(BF16), wglu_ref[...], preferred_element_type=F32) + bglu_ref[...]
        y_s5 = ys * _sigmoid(z)
        y_lru = (f32(hf_ref) + f32(hb_ref)) * _gelu(f32(gr_ref))
        out = jnp.dot(y_s5.astype(BF16), wout_ref[0:D_S5, :], preferred_element_type=F32)
        out = out + jnp.dot(y_lru.astype(BF16), wout_ref[D_S5:, :], preferred_element_type=F32)
        return load_x(rows) + _per_batch(_rms(out, gains_ref[1:2, :]), lambda t: t * gate1[None])

    def ffn(x):
        h = _per_batch(_rms(x, gains_ref[2:3, :]), lambda t: t * (1.0 + scale2)[None] + shift2[None])
        h = h.astype(BF16)
        f = jnp.zeros((TAIL_ROWS, D_MODEL), F32)
        for start, size in FF_CHUNKS:
            gt = jnp.dot(h, wfin_ref[:, start:start + size], preferred_element_type=F32)
            up = jnp.dot(h, wfin_ref[:, D_FF + start:D_FF + start + size], preferred_element_type=F32)
            act = (gt * _sigmoid(gt) * up).astype(BF16)
            f = f + jnp.dot(act, wfout_ref[start:start + size, :], preferred_element_type=F32)
        return x + _per_batch(_rms(f, gains_ref[3:4, :]), lambda t: t * gate2[None])

    xs = [mixer_out(rows) for rows in groups]
    xs = [ffn(x) for x in xs]
    for s, (rows, x) in enumerate(zip(groups, xs)):
        if has_next:
            modn_ref, gn_ref, win_ref, o_ref, un_ref, xrn_ref, grn_ref = refs
            o_ref[rows, :] = x
            _premix_math(x, modn_ref[0], gn_ref[0:1, :], win_ref,
                         un_ref.at[rows, :], xrn_ref.at[rows, :], grn_ref.at[rows, :])
        else:
            (o_ref,) = refs
            steps = slice(s * TAIL_ROWS // BATCH, (s + 1) * TAIL_ROWS // BATCH)
            o_ref[:, steps, :] = jnp.swapaxes(x.reshape(TAIL_ROWS // BATCH, BATCH, D_MODEL), 0, 1)


def _tail(l, x_src, u, yf, yb, gr, hf, hb, mod, d, wglu, bglu, wout, gains, wfin, wfout, w_in):
    first = len(x_src) == 2
    has_next = l < DEPTH - 1
    n_blk = N_BLK if has_next else N_LAT_BLK
    rows = n_blk * TM
    row = lambda i: (i, 0)
    once = pl.Buffered(1)
    wide = pl.BlockSpec((TM, D_MODEL), row)
    half = pl.BlockSpec((TM, D_S5), row)
    in_specs = ([_X_SPEC, _CTX_SPEC] if first else [wide]) + [half] * 6 + [
        _mod_spec(l),
        _layer_spec(l, (1, D_S5)),
        _layer_spec(l, (D_S5, D_S5), pipeline_mode=once),
        _layer_spec(l, (1, D_S5)),
        _layer_spec(l, (D_MODEL, D_MODEL), pipeline_mode=once),
        _layer_spec(l, (4, D_MODEL)),
        _layer_spec(l, (D_MODEL, 2 * D_FF), pipeline_mode=once),
        _layer_spec(l, (D_FF, D_MODEL), pipeline_mode=once),
    ]
    args = [*x_src, u, yf, yb, gr, hf, hb, mod, d, wglu, bglu, wout, gains, wfin, wfout]
    if has_next:
        in_specs += [_mod_spec(l + 1), _layer_spec(l + 1, (4, D_MODEL)),
                     _layer_spec(l + 1, (D_MODEL, D_IN), pipeline_mode=once)]
        args += [mod, gains, w_in]
        out_specs = [wide] + [half] * 3
        out_shape = [jax.ShapeDtypeStruct((rows, D_MODEL), F32)]
        out_shape += [jax.ShapeDtypeStruct((rows, D_S5), dt) for dt in (BF16, F32, BF16)]
    else:
        out_specs = [pl.BlockSpec((BATCH, TB, D_MODEL), lambda i: (0, i, 0))]
        out_shape = [jax.ShapeDtypeStruct((BATCH, SEQ, D_MODEL), F32)]
    return pl.pallas_call(
        functools.partial(_tail_kernel, first, has_next),
        grid=(n_blk,),
        in_specs=in_specs,
        out_specs=out_specs,
        out_shape=out_shape,
        compiler_params=_params(("parallel",)),
        name="tail",
    )(*args)


def _s5_matrices(a_re, a_im, log_dt, b_re, b_im, c_re, c_im):
    dt = jnp.exp(log_dt)[..., None]
    mag = jnp.exp(a_re * dt)
    ab_re = mag * jnp.cos(a_im * dt)
    ab_im = mag * jnp.sin(a_im * dt)
    den = a_re * a_re + a_im * a_im
    nr = ab_re - 1.0
    f_re = (nr * a_re + ab_im * a_im) / den
    f_im = (ab_im * a_re - nr * a_im) / den
    bb_re = f_re[..., None] * b_re - f_im[..., None] * b_im
    bb_im = f_re[..., None] * b_im + f_im[..., None] * b_re

    def compact(m):
        m = m.reshape(DEPTH, 2, N_SLAB, GROUPS_PER_SLAB, 2, 2, S5_GROUP, S5_STATE)
        return m.transpose(0, 1, 2, 4, 6, 5, 3, 7).reshape(DEPTH, 2, N_SLAB, 2 * S5_GROUP, 2 * SLAB_STATE)

    ar, ai = ab_re[..., None, :], ab_im[..., None, :]
    br, bi = jnp.swapaxes(bb_re, -1, -2), jnp.swapaxes(bb_im, -1, -2)
    first = jnp.stack([br * ar - bi * ai, br * ai + bi * ar], axis=3)
    c_in = compact(jnp.stack([first, jnp.stack([br, bi], axis=3)], axis=3))
    now = jnp.stack([c_re, -c_im], axis=3)
    ahead = jnp.stack([c_re * ar - c_im * ai, -(c_re * ai + c_im * ar)], axis=3)
    c_out = compact(jnp.stack([now, ahead], axis=3))

    hi = lax.Precision.HIGHEST
    d_group = (jnp.einsum('ldgnp,ldgqn->ldgpq', bb_re, c_re, precision=hi)
               - jnp.einsum('ldgnp,ldgqn->ldgpq', bb_im, c_im, precision=hi))
    d_group = d_group.reshape(DEPTH, 2, N_SLAB, GROUPS_PER_SLAB, S5_GROUP, S5_GROUP)
    w_d = jnp.einsum('ldjgpq,gh->ldjgphq', d_group, jnp.eye(GROUPS_PER_SLAB, dtype=F32))
    w_d = w_d.reshape(DEPTH, 2, N_SLAB, LANES, LANES).astype(BF16)

    tiles = lambda m: m.reshape(DEPTH, 2, N_SLAB, S5_TILES, LANES)
    a2 = jnp.stack([tiles(ab_re * ab_re - ab_im * ab_im), tiles(2.0 * ab_re * ab_im)], axis=4)
    a2 = jnp.broadcast_to(a2[..., None, :], (DEPTH, 2, N_SLAB, S5_TILES, 2, BATCH, LANES))
    return c_in, c_out, w_d, a2


def _lru_gate_matrix(w_rg, w_ig):
    heads = LRU_HALF // LRU_HEAD_DIM
    halves = D_LRU // LRU_HALF
    eye = jnp.eye(heads, dtype=F32)

    def bd(w):
        w = w.reshape(DEPTH, 2, halves, heads, LRU_HEAD_DIM, LRU_HEAD_DIM)
        return jnp.einsum('ldbhij,hk->ldbhikj', w, eye).reshape(DEPTH, 2, halves, LRU_HALF, LRU_HALF)

    return jnp.concatenate([bd(w_rg), bd(w_ig)], axis=-1).astype(BF16)


def kernel(x, c, ctx, c_ctx, w_ada, b_ada, norm_gains, w_in, s5_a_re, s5_a_im, s5_log_dt, s5_b_re, s5_b_im, s5_c_re, s5_c_im, s5_d, s5_w_glu, s5_b_glu, lru_conv_w, lru_conv_b, lru_w_rg, lru_b_rg, lru_w_ig, lru_b_ig, lru_lambda, w_out, w_ffn_in, w_ffn_out):
    assert x.shape == (BATCH, SEQ, D_MODEL) and ctx.shape == (BATCH, CTX_LEN, D_MODEL)
    cond = jnp.concatenate([c, c_ctx[None], jnp.zeros((2 * BATCH - BATCH - 1, D_MODEL), F32)], axis=0)
    m = _adaln(cond, w_ada, b_ada)
    mod = jnp.stack(
        [m[:, :BATCH], jnp.broadcast_to(m[:, BATCH:BATCH + 1], (DEPTH, BATCH, 6 * D_MODEL))], axis=1)

    w_in, w_glu, w_out, w_ffn_in, w_ffn_out = (
        w.astype(BF16) for w in (w_in, s5_w_glu, w_out, w_ffn_in, w_ffn_out))
    s5_w = _s5_matrices(s5_a_re, s5_a_im, s5_log_dt, s5_b_re, s5_b_im, s5_c_re, s5_c_im)
    lru_w = (0.5 * lru_conv_w, 0.5 * lru_conv_b[:, None], _lru_gate_matrix(lru_w_rg, lru_w_ig),
             0.5 * jnp.stack([lru_b_rg, lru_b_ig], axis=2), lru_lambda)

    u, xr, gr = _premix(0, x, ctx, mod, norm_gains, w_in)
    x_src = (x, ctx)
    for l in range(DEPTH):
        yf, yb = _s5(l, u, *s5_w)
        hf, hb = _lru(l, xr, lru_w)
        outs = _tail(l, x_src, u, yf, yb, gr, hf, hb, mod, s5_d[:, None], w_glu, s5_b_glu[:, None],
                     w_out, norm_gains, w_ffn_in, w_ffn_out, w_in)
        if l < DEPTH - 1:
            x_all, u, xr, gr = outs
            x_src = (x_all,)
    return outs[0]
```

```python
import functools
import math

import jax
import jax.numpy as jnp
from jax import lax
from jax.experimental import pallas as pl
from jax.experimental.pallas import tpu as pltpu

F32 = jnp.float32
BF16 = jnp.bfloat16

D_MODEL = 1024
BATCH = 8
SEQ = 4096
DEPTH = 4
GRID_W = 64
GRID_H = SEQ // GRID_W
CTX_LEN = 256
D_S5 = 512
S5_GROUP = 16
S5_GROUPS = 32
S5_STATE = 64
D_LRU = 512
LRU_HEADS = 8
LRU_HEAD_DIM = 64
CONV_W = 4
CONV_LEFT = 2
LRU_C = 8.0
D_IN = D_S5 + 2 * D_LRU
D_FF = 2816
EPS = 1e-6

LANES = 128
SUBLANES = 8
TB = 64
TM = TB * BATCH
N_LAT_BLK = SEQ // TB
N_CTX_BLK = CTX_LEN // TB
N_BLK = N_LAT_BLK + N_CTX_BLK
ROWS_LAT = SEQ * BATCH
ROWS_ALL = (SEQ + CTX_LEN) * BATCH
N_SLAB = D_S5 // LANES
GROUPS_PER_SLAB = LANES // S5_GROUP
SLAB_STATE = GROUPS_PER_SLAB * S5_STATE
S5_TILES = SLAB_STATE // LANES
S5_TB = 256
S5_TM = S5_TB * BATCH
S5_N_LAT = SEQ // S5_TB
S5_N_CTX = CTX_LEN // S5_TB
S5_N_BLK = S5_N_LAT + S5_N_CTX
LRU_HALF = 256
LRU_RUN = GRID_H
LRU_COLS = 2
assert TB == LRU_RUN and GRID_W % LRU_COLS == 0
FF_CHUNKS = ((0, 1024), (1024, 1024), (2048, 768))
TAIL_SPLIT = 2
TAIL_ROWS = TM // TAIL_SPLIT
VMEM_LIMIT = 58 * 1024 * 1024


def _params(sem):
    return pltpu.CompilerParams(dimension_semantics=sem, vmem_limit_bytes=VMEM_LIMIT)


def _rms(x, g):
    var = jnp.mean(x * x, axis=-1, keepdims=True)
    return x * lax.rsqrt(var + EPS) * g


def _per_batch(x, fn):
    r, ch = x.shape
    return fn(x.reshape(r // BATCH, BATCH, ch)).reshape(r, ch)


def _gelu(x):
    return jax.nn.gelu(x, approximate=True)


def _sigmoid(x):
    return 0.5 * jnp.tanh(0.5 * x) + 0.5


def _adaln_kernel(c_ref, w_ref, b_ref, o_ref):
    c = c_ref[...]
    s = (c * jax.nn.sigmoid(c)).astype(BF16)
    o_ref[0] = jnp.dot(s, w_ref[0].astype(BF16), preferred_element_type=F32) + b_ref[0]


def _adaln(cond, w_ada, b_ada):
    rows = cond.shape[0]
    nb = 6
    return pl.pallas_call(
        _adaln_kernel,
        grid=(DEPTH, nb),
        in_specs=[
            pl.BlockSpec((rows, D_MODEL), lambda l, n: (0, 0)),
            pl.BlockSpec((1, D_MODEL, D_MODEL), lambda l, n: (l, 0, n)),
            pl.BlockSpec((1, 1, D_MODEL), lambda l, n: (l, 0, n)),
        ],
        out_specs=pl.BlockSpec((1, rows, D_MODEL), lambda l, n: (l, 0, n)),
        out_shape=jax.ShapeDtypeStruct((DEPTH, rows, 6 * D_MODEL), F32),
        compiler_params=_params(("arbitrary", "arbitrary")),
        name="adaln",
    )(cond, w_ada, b_ada.reshape(DEPTH, 1, 6 * D_MODEL))


def _mod_index(i):
    return (i >= N_LAT_BLK).astype(jnp.int32)


def _layer_spec(l, shape, **kw):
    zeros = (0,) * len(shape)
    return pl.BlockSpec((None,) + tuple(shape), lambda i: (l,) + zeros, **kw)


def _mod_spec(l):
    return pl.BlockSpec((None, 1, BATCH, 6 * D_MODEL), lambda i: (l, _mod_index(i), 0, 0))


def _premix_math(x, mod, g, w_ref, u_ref, xr_ref, gr_ref):
    shift, scale = mod[:, 0:D_MODEL], mod[:, D_MODEL:2 * D_MODEL]
    h = _per_batch(_rms(x, g), lambda t: t * (1.0 + scale)[None] + shift[None])
    p = jnp.dot(h.astype(BF16), w_ref[...], preferred_element_type=F32)
    u_ref[...] = p[:, 0:D_S5].astype(u_ref.dtype)
    xr_ref[...] = p[:, D_S5:D_S5 + D_LRU]
    gr_ref[...] = p[:, D_S5 + D_LRU:].astype(gr_ref.dtype)


def _input_rows(x_ref, ctx_ref):
    blk = jnp.where(pl.program_id(0) < N_LAT_BLK, x_ref[...], ctx_ref[...])
    return jnp.swapaxes(blk, 0, 1).reshape(TM, D_MODEL)


_X_SPEC = pl.BlockSpec((BATCH, TB, D_MODEL), lambda i: (0, jnp.minimum(i, N_LAT_BLK - 1), 0))
_CTX_SPEC = pl.BlockSpec((BATCH, TB, D_MODEL), lambda i: (0, jnp.maximum(i - N_LAT_BLK, 0), 0))


def _premix_kernel(x_ref, ctx_ref, mod_ref, g_ref, w_ref, u_ref, xr_ref, gr_ref):
    _premix_math(_input_rows(x_ref, ctx_ref), mod_ref[0], g_ref[0:1, :], w_ref, u_ref, xr_ref, gr_ref)


def _premix(l, x, ctx, mod, gains, w_in):
    row = lambda i: (i, 0)
    half = lambda dtype: jax.ShapeDtypeStruct((ROWS_ALL, D_S5), dtype)
    return pl.pallas_call(
        _premix_kernel,
        grid=(N_BLK,),
        in_specs=[
            _X_SPEC,
            _CTX_SPEC,
            _mod_spec(l),
            _layer_spec(l, (4, D_MODEL)),
            _layer_spec(l, (D_MODEL, D_IN)),
        ],
        out_specs=[pl.BlockSpec((TM, D_S5), row)] * 3,
        out_shape=[half(BF16), half(F32), half(BF16)],
        compiler_params=_params(("parallel",)),
        name="premix",
    )(x, ctx, mod, gains, w_in)


def _s5_expand(compact):
    rows, cols = 2 * LANES, 2 * SLAB_STATE
    full = jnp.broadcast_to(compact.reshape(2, 1, S5_GROUP, cols), (2, GROUPS_PER_SLAB, S5_GROUP, cols))
    row_group = (lax.broadcasted_iota(jnp.int32, (rows, cols), 0) // S5_GROUP) % GROUPS_PER_SLAB
    col_group = (lax.broadcasted_iota(jnp.int32, (rows, cols), 1) // S5_STATE) % GROUPS_PER_SLAB
    return jnp.where(row_group == col_group, full.reshape(rows, cols), 0.0)


def _s5_kernel(uf_ref, ub_ref, cin_ref, cout_ref, wd_ref, a2_ref, yf_ref, yb_ref,
               win_scr, wout_scr, st_scr, carry_scr):
    @pl.when(pl.program_id(0) == 0)
    def _():
        st_scr[...] = jnp.zeros_like(st_scr)
        carry_scr[...] = jnp.zeros_like(carry_scr)
        for d in range(2):
            for j in range(N_SLAB):
                win_scr[d, j] = _s5_expand(cin_ref[d, j]).astype(BF16)
                wout_scr[d, j] = _s5_expand(cout_ref[d, j]).T.astype(BF16)

    def tile(m, piece):
        return [piece(slice(m * LANES, (m + 1) * LANES)),
                piece(slice(SLAB_STATE + m * LANES, SLAB_STATE + (m + 1) * LANES))]

    u_refs = (uf_ref, ub_ref)
    y_refs = (yf_ref, yb_ref)
    np_ = S5_TB // 2
    dot = lambda p, q: lax.dot_general(p, q, (((1,), (0,)), ((), ())), preferred_element_type=F32)
    for j in range(N_SLAB):
        lanes = slice(j * LANES, (j + 1) * LANES)
        for d in range(2):
            u = u_refs[d][:, lanes].astype(F32).reshape(np_, 2, BATCH, LANES)
            even, odd = u[:, 0].reshape(np_ * BATCH, LANES), u[:, 1].reshape(np_ * BATCH, LANES)
            lhs = jnp.concatenate([even, odd] if d == 0 else [odd, even], axis=1)
            out = None
            for m in range(S5_TILES):
                w_in = jnp.concatenate(tile(m, lambda c: win_scr[d, j, :, c]), axis=1)
                w_out = jnp.concatenate(tile(m, lambda c: wout_scr[d, j, c, :]), axis=0)
                z = dot(lhs, w_in)
                ar, ai = a2_ref[d, j, m, 0], a2_ref[d, j, m, 1]
                sr, si = st_scr[d, j, m, 0], st_scr[d, j, m, 1]
                states = [None] * np_
                for p in range(np_):
                    r = np_ - 1 - p if d == 1 else p
                    zr = z[r * BATCH:(r + 1) * BATCH]
                    sr, si = (ar * sr - ai * si + zr[:, 0:LANES],
                              ar * si + ai * sr + zr[:, LANES:])
                    states[r] = jnp.concatenate([sr, si], axis=1)
                st_scr[d, j, m, 0] = sr
                st_scr[d, j, m, 1] = si
                part = dot(jnp.concatenate(states, axis=0), w_out)
                out = part if out is None else out + part
            direct, ahead = out[:, 0:LANES], out[:, LANES:]
            carry = carry_scr[d, j]
            if d == 0:
                carry_scr[d, j] = ahead[(np_ - 1) * BATCH:]
                y_even = jnp.concatenate([carry, ahead[:(np_ - 1) * BATCH]], axis=0) + dot(even, wd_ref[d, j])
                y_odd = direct
            else:
                carry_scr[d, j] = ahead[:BATCH]
                y_odd = jnp.concatenate([ahead[BATCH:], carry], axis=0) + dot(odd, wd_ref[d, j])
                y_even = direct
            y = jnp.stack([y_even.reshape(np_, BATCH, LANES), y_odd.reshape(np_, BATCH, LANES)], axis=1)
            y_refs[d][:, lanes] = y.reshape(S5_TM, LANES).astype(y_refs[d].dtype)


def _s5_fwd_block(i):
    return jnp.where(i < S5_N_CTX, S5_N_LAT + i, i - S5_N_CTX)


def _s5_bwd_block(i):
    return S5_N_BLK - 1 - i


def _s5(l, u, c_in, c_out, w_d, a2):
    out = jax.ShapeDtypeStruct((ROWS_ALL, D_S5), BF16)
    return pl.pallas_call(
        _s5_kernel,
        grid=(S5_N_BLK,),
        in_specs=[
            pl.BlockSpec((S5_TM, D_S5), lambda i: (_s5_fwd_block(i), 0)),
            pl.BlockSpec((S5_TM, D_S5), lambda i: (_s5_bwd_block(i), 0)),
            _layer_spec(l, (2, N_SLAB, 2 * S5_GROUP, 2 * SLAB_STATE)),
            _layer_spec(l, (2, N_SLAB, 2 * S5_GROUP, 2 * SLAB_STATE)),
            _layer_spec(l, (2, N_SLAB, LANES, LANES)),
            _layer_spec(l, (2, N_SLAB, S5_TILES, 2, BATCH, LANES)),
        ],
        out_specs=[
            pl.BlockSpec((S5_TM, D_S5), lambda i: (_s5_fwd_block(i), 0)),
            pl.BlockSpec((S5_TM, D_S5), lambda i: (_s5_bwd_block(i), 0)),
        ],
        out_shape=[out, out],
        scratch_shapes=[pltpu.VMEM((2, N_SLAB, 2 * LANES, 2 * SLAB_STATE), BF16),
                        pltpu.VMEM((2, N_SLAB, 2 * SLAB_STATE, 2 * LANES), BF16),
                        pltpu.VMEM((2, N_SLAB, S5_TILES, 2, BATCH, LANES), F32),
                        pltpu.VMEM((2, N_SLAB, BATCH, LANES), F32)],
        compiler_params=_params(("arbitrary",)),
        name="s5_scan",
    )(u, u, c_in, c_out, w_d, a2)


def _lru_conv(xe, n, cw_ref, cb_ref):
    out = xe[0:n] * cw_ref[0:1, :][None]
    for k in range(1, CONV_W):
        out = out + xe[k:k + n] * cw_ref[k:k + 1, :][None]
    return out + cb_ref[...][None]


def _lru_coeffs(xc, d, wg_ref, bg_ref, lam_ref, a_scr, b_scr):
    n = xc.shape[0]
    x2 = xc.reshape(n * BATCH, D_LRU)
    for hb in range(D_LRU // LRU_HALF):
        sl = slice(hb * LRU_HALF, (hb + 1) * LRU_HALF)
        xh = x2[:, sl]
        g = jnp.dot(xh.astype(BF16), wg_ref[d, hb], preferred_element_type=F32)
        tr = jnp.tanh(g[:, 0:LRU_HALF] + bg_ref[d, 0:1, sl])
        ti = jnp.tanh(g[:, LRU_HALF:] + bg_ref[d, 1:2, sl])
        lam = lam_ref[d:d + 1, sl]
        softplus = jnp.maximum(-lam, 0.0) + jnp.log1p(jnp.exp(-jnp.abs(lam)))
        k = (-0.5 * LRU_C * math.log2(math.e)) * softplus
        a = jnp.exp2(k * tr + k)
        rest = 1.0 - a * a
        mult = jnp.where(rest > 0.0, rest * lax.rsqrt(rest), 0.0)
        b = mult * (ti * xh + xh)
        a_scr[d, :, :, sl] = a.reshape(n, BATCH, LRU_HALF)
        b_scr[d, :, :, sl] = b.reshape(n, BATCH, LRU_HALF)


def _lru_scan(n, hf0, hb0, a_scr, b_scr, write_f, write_b):
    def body(r, carry):
        hf, hb = carry
        rb = n - 1 - r
        hf = a_scr[0, r] * hf + b_scr[0, r]
        hb = a_scr[1, rb] * hb + b_scr[1, rb]
        write_f(r, hf)
        write_b(rb, hb)
        return hf, hb
    return lax.fori_loop(0, n, body, (hf0, hb0), unroll=4)


def _lru_kernel(n_runs, cols, n_aliased, xf_ref, xfp_ref, xfn_ref, xb_ref, xbp_ref, xbn_ref,
                cw_ref, cb_ref, wg_ref, bg_ref, lam_ref, st0_ref, *refs):
    hf_ref, hb_ref, st_ref, a_scr, b_scr = refs[n_aliased:]
    i = pl.program_id(0)
    n = LRU_RUN * max(cols, 1)

    @pl.when(i == 0)
    def _():
        st_ref[...] = st0_ref[...]

    def run(x_ref, prev_ref, next_ref, w):
        has_prev = (w > 0).astype(F32)
        has_next = (w < n_runs - 1).astype(F32)
        xs = [x_ref[...]] if cols == 0 else [x_ref[:, c] for c in range(cols)]
        xe = jnp.concatenate([prev_ref[...] * has_prev] + xs + [next_ref[...] * has_next], axis=0)
        return _lru_conv(xe, n, cw_ref, cb_ref)

    _lru_coeffs(run(xf_ref, xfp_ref, xfn_ref, i), 0, wg_ref, bg_ref, lam_ref, a_scr, b_scr)
    _lru_coeffs(run(xb_ref, xbp_ref, xbn_ref, n_runs - 1 - i), 1, wg_ref, bg_ref, lam_ref,
                a_scr, b_scr)

    def write(ref, t, h):
        if cols == 0:
            ref[t] = h
        else:
            ref[lax.rem(t, GRID_H), lax.div(t, GRID_H)] = h

    write_f = functools.partial(write, hf_ref)
    write_b = functools.partial(write, hb_ref)

    hf, hb = _lru_scan(n, st_ref[0], st_ref[1], a_scr, b_scr, write_f, write_b)
    st_ref[0] = hf
    st_ref[1] = hb


def _lru_call(name, l, n_runs, cols, run_spec, prev_spec, next_spec, x4, weights, st0, hf_in, hb_in):
    full = jax.ShapeDtypeStruct((N_BLK, TB, BATCH, D_LRU), F32)
    st_spec = pl.BlockSpec((2, BATCH, D_LRU), lambda i: (0, 0, 0))
    wspecs = [
        _layer_spec(l, (CONV_W, D_LRU)),
        _layer_spec(l, (1, D_LRU)),
        _layer_spec(l, (2, 2, LRU_HALF, 2 * LRU_HALF)),
        _layer_spec(l, (2, 2, D_LRU)),
        _layer_spec(l, (2, D_LRU)),
    ]
    fw = lambda i: i
    bw = lambda i: n_runs - 1 - i
    aliased = [] if hf_in is None else [hf_in, hb_in]
    n_in = 6 + len(wspecs) + 1
    return pl.pallas_call(
        functools.partial(_lru_kernel, n_runs, cols, len(aliased)),
        grid=(n_runs,),
        in_specs=[run_spec(fw), prev_spec(fw), next_spec(fw),
                  run_spec(bw), prev_spec(bw), next_spec(bw)] + wspecs
        + [st_spec] + [pl.BlockSpec(memory_space=pl.ANY)] * len(aliased),
        out_specs=[run_spec(fw), run_spec(bw), st_spec],
        out_shape=[full, full, jax.ShapeDtypeStruct((2, BATCH, D_LRU), F32)],
        scratch_shapes=[pltpu.VMEM((2, LRU_RUN * max(cols, 1), BATCH, D_LRU), F32)] * 2,
        input_output_aliases={n_in + k: k for k in range(len(aliased))},
        compiler_params=_params(("arbitrary",)),
        name=name,
    )(x4, x4, x4, x4, x4, x4, *weights, st0, *aliased)


def _lru(l, xr, weights):
    x4 = xr.reshape(N_BLK, TB, BATCH, D_LRU)
    st0 = jnp.zeros((2, BATCH, D_LRU), F32)

    c0 = N_LAT_BLK
    clip = lambda k, n: jnp.clip(k, 0, n - 1)
    ctx_run = lambda f: pl.BlockSpec((None, TB, BATCH, D_LRU), lambda i: (c0 + f(i), 0, 0, 0))
    ctx_prev = lambda f: pl.BlockSpec(
        (None, CONV_LEFT, BATCH, D_LRU),
        lambda i: (c0 + clip(f(i) - 1, N_CTX_BLK), TB // CONV_LEFT - 1, 0, 0))
    ctx_next = lambda f: pl.BlockSpec(
        (None, 1, BATCH, D_LRU), lambda i: (c0 + clip(f(i) + 1, N_CTX_BLK), 0, 0, 0))
    hf, hb, st = _lru_call("lru_ctx", l, N_CTX_BLK, 0, ctx_run, ctx_prev, ctx_next,
                           x4, weights, st0, None, None)

    lat_run = lambda f: pl.BlockSpec((GRID_H, LRU_COLS, BATCH, D_LRU), lambda i: (0, f(i), 0, 0))
    lat_prev = lambda f: pl.BlockSpec(
        (CONV_LEFT, None, BATCH, D_LRU),
        lambda i: (GRID_H // CONV_LEFT - 1, clip(f(i) * LRU_COLS - 1, GRID_W), 0, 0))
    lat_next = lambda f: pl.BlockSpec(
        (1, None, BATCH, D_LRU), lambda i: (0, clip((f(i) + 1) * LRU_COLS, GRID_W), 0, 0))
    hf, hb, _ = _lru_call("lru_lat", l, GRID_W // LRU_COLS, LRU_COLS, lat_run, lat_prev, lat_next,
                          x4, weights, st, hf, hb)
    return hf.reshape(ROWS_ALL, D_LRU), hb.reshape(ROWS_ALL, D_LRU)


def _tail_kernel(first, has_next, *refs):
    if first:
        x_in = _input_rows(refs[0], refs[1])
        load_x = lambda rows: x_in[rows]
        refs = refs[2:]
    else:
        x_ref = refs[0]
        load_x = lambda rows: x_ref[rows, :]
        refs = refs[1:]
    (u_ref, yf_ref, yb_ref, gr_ref, hf_ref, hb_ref, mod_ref,
     d_ref, wglu_ref, bglu_ref, wout_ref, gains_ref, wfin_ref, wfout_ref) = refs[:14]
    refs = refs[14:]
    mod = mod_ref[0]
    chunk = lambda k: mod[:, k * D_MODEL:(k + 1) * D_MODEL]
    gate1, shift2, scale2, gate2 = chunk(2), chunk(3), chunk(4), chunk(5)
    groups = [slice(s * TAIL_ROWS, (s + 1) * TAIL_ROWS) for s in range(TAIL_SPLIT)]

    def mixer_out(rows):
        f32 = lambda ref: ref[rows, :].astype(F32)
        ys = _gelu(d_ref[...] * f32(u_ref) + f32(yf_ref) + f32(yb_ref))
        z = jnp.dot(ys.astype(BF16), wglu_ref[...], preferred_element_type=F32) + bglu_ref[...]
        y_s5 = ys * _sigmoid(z)
        y_lru = (f32(hf_ref) + f32(hb_ref)) * _gelu(f32(gr_ref))
        out = jnp.dot(y_s5.astype(BF16), wout_ref[0:D_S5, :], preferred_element_type=F32)
        out = out + jnp.dot(y_lru.astype(BF16), wout_ref[D_S5:, :], preferred_element_type=F32)
        return load_x(rows) + _per_batch(_rms(out, gains_ref[1:2, :]), lambda t: t * gate1[None])

    def ffn(x):
        h = _per_batch(_rms(x, gains_ref[2:3, :]), lambda t: t * (1.0 + scale2)[None] + shift2[None])
        h = h.astype(BF16)
        f = jnp.zeros((TAIL_ROWS, D_MODEL), F32)
        for start, size in FF_CHUNKS:
            gt = jnp.dot(h, wfin_ref[:, start:start + size], preferred_element_type=F32)
            up = jnp.dot(h, wfin_ref[:, D_FF + start:D_FF + start + size], preferred_element_type=F32)
            act = (gt * _sigmoid(gt) * up).astype(BF16)
            f = f + jnp.dot(act, wfout_ref[start:start + size, :], preferred_element_type=F32)
        return x + _per_batch(_rms(f, gains_ref[3:4, :]), lambda t: t * gate2[None])

    xs = [mixer_out(rows) for rows in groups]
    xs = [ffn(x) for x in xs]
    for s, (rows, x) in enumerate(zip(groups, xs)):
        if has_next:
            modn_ref, gn_ref, win_ref, o_ref, un_ref, xrn_ref, grn_ref = refs
            o_ref[rows, :] = x
            _premix_math(x, modn_ref[0], gn_ref[0:1, :], win_ref,
                         un_ref.at[rows, :], xrn_ref.at[rows, :], grn_ref.at[rows, :])
        else:
            (o_ref,) = refs
            steps = slice(s * TAIL_ROWS // BATCH, (s + 1) * TAIL_ROWS // BATCH)
            o_ref[:, steps, :] = jnp.swapaxes(x.reshape(TAIL_ROWS // BATCH, BATCH, D_MODEL), 0, 1)


def _tail(l, x_src, u, yf, yb, gr, hf, hb, mod, d, wglu, bglu, wout, gains, wfin, wfout, w_in):
    first = len(x_src) == 2
    has_next = l < DEPTH - 1
    n_blk = N_BLK if has_next else N_LAT_BLK
    rows = n_blk * TM
    row = lambda i: (i, 0)
    once = pl.Buffered(1)
    wide = pl.BlockSpec((TM, D_MODEL), row)
    half = pl.BlockSpec((TM, D_S5), row)
    in_specs = ([_X_SPEC, _CTX_SPEC] if first else [wide]) + [half] * 6 + [
        _mod_spec(l),
        _layer_spec(l, (1, D_S5)),
        _layer_spec(l, (D_S5, D_S5), pipeline_mode=once),
        _layer_spec(l, (1, D_S5)),
        _layer_spec(l, (D_MODEL, D_MODEL), pipeline_mode=once),
        _layer_spec(l, (4, D_MODEL)),
        _layer_spec(l, (D_MODEL, 2 * D_FF), pipeline_mode=once),
        _layer_spec(l, (D_FF, D_MODEL), pipeline_mode=once),
    ]
    args = [*x_src, u, yf, yb, gr, hf, hb, mod, d, wglu, bglu, wout, gains, wfin, wfout]
    if has_next:
        in_specs += [_mod_spec(l + 1), _layer_spec(l + 1, (4, D_MODEL)),
                     _layer_spec(l + 1, (D_MODEL, D_IN), pipeline_mode=once)]
        args += [mod, gains, w_in]
        out_specs = [wide] + [half] * 3
        out_shape = [jax.ShapeDtypeStruct((rows, D_MODEL), F32)]
        out_shape += [jax.ShapeDtypeStruct((rows, D_S5), dt) for dt in (BF16, F32, BF16)]
    else:
        out_specs = [pl.BlockSpec((BATCH, TB, D_MODEL), lambda i: (0, i, 0))]
        out_shape = [jax.ShapeDtypeStruct((BATCH, SEQ, D_MODEL), F32)]
    return pl.pallas_call(
        functools.partial(_tail_kernel, first, has_next),
        grid=(n_blk,),
        in_specs=in_specs,
        out_specs=out_specs,
        out_shape=out_shape,
        compiler_params=_params(("parallel",)),
        name="tail",
    )(*args)


def _s5_matrices(a_re, a_im, log_dt, b_re, b_im, c_re, c_im):
    dt = jnp.exp(log_dt)[..., None]
    mag = jnp.exp(a_re * dt)
    ab_re = mag * jnp.cos(a_im * dt)
    ab_im = mag * jnp.sin(a_im * dt)
    den = a_re * a_re + a_im * a_im
    nr = ab_re - 1.0
    f_re = (nr * a_re + ab_im * a_im) / den
    f_im = (ab_im * a_re - nr * a_im) / den
    bb_re = f_re[..., None] * b_re - f_im[..., None] * b_im
    bb_im = f_re[..., None] * b_im + f_im[..., None] * b_re

    def compact(m):
        m = m.reshape(DEPTH, 2, N_SLAB, GROUPS_PER_SLAB, 2, 2, S5_GROUP, S5_STATE)
        return m.transpose(0, 1, 2, 4, 6, 5, 3, 7).reshape(DEPTH, 2, N_SLAB, 2 * S5_GROUP, 2 * SLAB_STATE)

    ar, ai = ab_re[..., None, :], ab_im[..., None, :]
    br, bi = jnp.swapaxes(bb_re, -1, -2), jnp.swapaxes(bb_im, -1, -2)
    first = jnp.stack([br * ar - bi * ai, br * ai + bi * ar], axis=3)
    c_in = compact(jnp.stack([first, jnp.stack([br, bi], axis=3)], axis=3))
    now = jnp.stack([c_re, -c_im], axis=3)
    ahead = jnp.stack([c_re * ar - c_im * ai, -(c_re * ai + c_im * ar)], axis=3)
    c_out = compact(jnp.stack([now, ahead], axis=3))

    hi = lax.Precision.HIGHEST
    d_group = (jnp.einsum('ldgnp,ldgqn->ldgpq', bb_re, c_re, precision=hi)
               - jnp.einsum('ldgnp,ldgqn->ldgpq', bb_im, c_im, precision=hi))
    d_group = d_group.reshape(DEPTH, 2, N_SLAB, GROUPS_PER_SLAB, S5_GROUP, S5_GROUP)
    w_d = jnp.einsum('ldjgpq,gh->ldjgphq', d_group, jnp.eye(GROUPS_PER_SLAB, dtype=F32))
    w_d = w_d.reshape(DEPTH, 2, N_SLAB, LANES, LANES).astype(BF16)

    tiles = lambda m: m.reshape(DEPTH, 2, N_SLAB, S5_TILES, LANES)
    a2 = jnp.stack([tiles(ab_re * ab_re - ab_im * ab_im), tiles(2.0 * ab_re * ab_im)], axis=4)
    a2 = jnp.broadcast_to(a2[..., None, :], (DEPTH, 2, N_SLAB, S5_TILES, 2, BATCH, LANES))
    return c_in, c_out, w_d, a2


def _lru_gate_matrix(w_rg, w_ig):
    heads = LRU_HALF // LRU_HEAD_DIM
    halves = D_LRU // LRU_HALF
    eye = jnp.eye(heads, dtype=F32)

    def bd(w):
        w = w.reshape(DEPTH, 2, halves, heads, LRU_HEAD_DIM, LRU_HEAD_DIM)
        return jnp.einsum('ldbhij,hk->ldbhikj', w, eye).reshape(DEPTH, 2, halves, LRU_HALF, LRU_HALF)

    return jnp.concatenate([bd(w_rg), bd(w_ig)], axis=-1).astype(BF16)


def kernel(x, c, ctx, c_ctx, w_ada, b_ada, norm_gains, w_in, s5_a_re, s5_a_im, s5_log_dt, s5_b_re, s5_b_im, s5_c_re, s5_c_im, s5_d, s5_w_glu, s5_b_glu, lru_conv_w, lru_conv_b, lru_w_rg, lru_b_rg, lru_w_ig, lru_b_ig, lru_lambda, w_out, w_ffn_in, w_ffn_out):
    assert x.shape == (BATCH, SEQ, D_MODEL) and ctx.shape == (BATCH, CTX_LEN, D_MODEL)
    cond = jnp.concatenate([c, c_ctx[None], jnp.zeros((2 * BATCH - BATCH - 1, D_MODEL), F32)], axis=0)
    m = _adaln(cond, w_ada, b_ada)
    mod = jnp.stack(
        [m[:, :BATCH], jnp.broadcast_to(m[:, BATCH:BATCH + 1], (DEPTH, BATCH, 6 * D_MODEL))], axis=1)

    w_in, w_glu, w_out, w_ffn_in, w_ffn_out = (
        w.astype(BF16) for w in (w_in, s5_w_glu, w_out, w_ffn_in, w_ffn_out))
    s5_w = _s5_matrices(s5_a_re, s5_a_im, s5_log_dt, s5_b_re, s5_b_im, s5_c_re, s5_c_im)
    lru_w = (0.5 * lru_conv_w, 0.5 * lru_conv_b[:, None], _lru_gate_matrix(lru_w_rg, lru_w_ig),
             0.5 * jnp.stack([lru_b_rg, lru_b_ig], axis=2), lru_lambda)

    u, xr, gr = _premix(0, x, ctx, mod, norm_gains, w_in)
    x_src = (x, ctx)
    for l in range(DEPTH):
        yf, yb = _s5(l, u, *s5_w)
        hf, hb = _lru(l, xr, lru_w)
        outs = _tail(l, x_src, u, yf, yb, gr, hf, hb, mod, s5_d[:, None], w_glu, s5_b_glu[:, None],
                     w_out, norm_gains, w_ffn_in, w_ffn_out, w_in)
        if l < DEPTH - 1:
            x_all, u, xr, gr = outs
            x_src = (x_all,)
    return outs[0]
```

```python
import functools
import math

import jax
import jax.numpy as jnp
from jax import lax
from jax.experimental import pallas as pl
from jax.experimental.pallas import tpu as pltpu

F32 = jnp.float32
BF16 = jnp.bfloat16

D_MODEL = 1024
BATCH = 8
SEQ = 4096
DEPTH = 4
GRID_W = 64
GRID_H = SEQ // GRID_W
CTX_LEN = 256
D_S5 = 512
S5_GROUP = 16
S5_GROUPS = 32
S5_STATE = 64
D_LRU = 512
LRU_HEADS = 8
LRU_HEAD_DIM = 64
CONV_W = 4
CONV_LEFT = 2
LRU_C = 8.0
D_IN = D_S5 + 2 * D_LRU
D_FF = 2816
EPS = 1e-6

LANES = 128
SUBLANES = 8
TB = 64
TM = TB * BATCH
N_LAT_BLK = SEQ // TB
N_CTX_BLK = CTX_LEN // TB
N_BLK = N_LAT_BLK + N_CTX_BLK
ROWS_LAT = SEQ * BATCH
ROWS_ALL = (SEQ + CTX_LEN) * BATCH
N_SLAB = D_S5 // LANES
GROUPS_PER_SLAB = LANES // S5_GROUP
SLAB_STATE = GROUPS_PER_SLAB * S5_STATE
S5_TILES = SLAB_STATE // LANES
S5_TB = 256
S5_TM = S5_TB * BATCH
S5_N_LAT = SEQ // S5_TB
S5_N_CTX = CTX_LEN // S5_TB
S5_N_BLK = S5_N_LAT + S5_N_CTX
LRU_HALF = 256
LRU_RUN = GRID_H
LRU_COLS = 1
assert TB == LRU_RUN and GRID_W % LRU_COLS == 0
FF_CHUNKS = ((0, 1024), (1024, 1024), (2048, 768))
TAIL_SPLIT = 2
TAIL_ROWS = TM // TAIL_SPLIT
VMEM_LIMIT = 58 * 1024 * 1024


def _params(sem):
    return pltpu.CompilerParams(dimension_semantics=sem, vmem_limit_bytes=VMEM_LIMIT)


def _rms(x, g):
    var = jnp.mean(x * x, axis=-1, keepdims=True)
    return x * lax.rsqrt(var + EPS) * g


def _per_batch(x, fn):
    r, ch = x.shape
    return fn(x.reshape(r // BATCH, BATCH, ch)).reshape(r, ch)


def _gelu(x):
    return jax.nn.gelu(x, approximate=True)


def _sigmoid(x):
    return 0.5 * jnp.tanh(0.5 * x) + 0.5


def _adaln_kernel(c_ref, w_ref, b_ref, o_ref):
    c = c_ref[...]
    s = (c * jax.nn.sigmoid(c)).astype(BF16)
    o_ref[0] = jnp.dot(s, w_ref[0].astype(BF16), preferred_element_type=F32) + b_ref[0]


def _adaln(cond, w_ada, b_ada):
    rows = cond.shape[0]
    nb = 6
    return pl.pallas_call(
        _adaln_kernel,
        grid=(DEPTH, nb),
        in_specs=[
            pl.BlockSpec((rows, D_MODEL), lambda l, n: (0, 0)),
            pl.BlockSpec((1, D_MODEL, D_MODEL), lambda l, n: (l, 0, n)),
            pl.BlockSpec((1, 1, D_MODEL), lambda l, n: (l, 0, n)),
        ],
        out_specs=pl.BlockSpec((1, rows, D_MODEL), lambda l, n: (l, 0, n)),
        out_shape=jax.ShapeDtypeStruct((DEPTH, rows, 6 * D_MODEL), F32),
        compiler_params=_params(("arbitrary", "arbitrary")),
        name="adaln",
    )(cond, w_ada, b_ada.reshape(DEPTH, 1, 6 * D_MODEL))


def _mod_index(i):
    return (i >= N_LAT_BLK).astype(jnp.int32)


def _layer_spec(l, shape, **kw):
    zeros = (0,) * len(shape)
    return pl.BlockSpec((None,) + tuple(shape), lambda i: (l,) + zeros, **kw)


def _mod_spec(l):
    return pl.BlockSpec((None, 1, BATCH, 6 * D_MODEL), lambda i: (l, _mod_index(i), 0, 0))


def _premix_math(x, mod, g, w_ref, u_ref, xr_ref, gr_ref):
    shift, scale = mod[:, 0:D_MODEL], mod[:, D_MODEL:2 * D_MODEL]
    h = _per_batch(_rms(x, g), lambda t: t * (1.0 + scale)[None] + shift[None])
    p = jnp.dot(h.astype(BF16), w_ref[...], preferred_element_type=F32)
    u_ref[...] = p[:, 0:D_S5].astype(u_ref.dtype)
    xr_ref[...] = p[:, D_S5:D_S5 + D_LRU]
    gr_ref[...] = p[:, D_S5 + D_LRU:].astype(gr_ref.dtype)


def _input_rows(x_ref, ctx_ref):
    blk = jnp.where(pl.program_id(0) < N_LAT_BLK, x_ref[...], ctx_ref[...])
    return jnp.swapaxes(blk, 0, 1).reshape(TM, D_MODEL)


_X_SPEC = pl.BlockSpec((BATCH, TB, D_MODEL), lambda i: (0, jnp.minimum(i, N_LAT_BLK - 1), 0))
_CTX_SPEC = pl.BlockSpec((BATCH, TB, D_MODEL), lambda i: (0, jnp.maximum(i - N_LAT_BLK, 0), 0))


def _premix_kernel(x_ref, ctx_ref, mod_ref, g_ref, w_ref, u_ref, xr_ref, gr_ref):
    _premix_math(_input_rows(x_ref, ctx_ref), mod_ref[0], g_ref[0:1, :], w_ref, u_ref, xr_ref, gr_ref)


def _premix(l, x, ctx, mod, gains, w_in):
    row = lambda i: (i, 0)
    half = lambda dtype: jax.ShapeDtypeStruct((ROWS_ALL, D_S5), dtype)
    return pl.pallas_call(
        _premix_kernel,
        grid=(N_BLK,),
        in_specs=[
            _X_SPEC,
            _CTX_SPEC,
            _mod_spec(l),
            _layer_spec(l, (4, D_MODEL)),
            _layer_spec(l, (D_MODEL, D_IN)),
        ],
        out_specs=[pl.BlockSpec((TM, D_S5), row)] * 3,
        out_shape=[half(BF16), half(F32), half(BF16)],
        compiler_params=_params(("parallel",)),
        name="premix",
    )(x, ctx, mod, gains, w_in)


def _s5_expand(compact):
    rows, cols = 2 * LANES, 2 * SLAB_STATE
    full = jnp.broadcast_to(compact.reshape(2, 1, S5_GROUP, cols), (2, GROUPS_PER_SLAB, S5_GROUP, cols))
    row_group = (lax.broadcasted_iota(jnp.int32, (rows, cols), 0) // S5_GROUP) % GROUPS_PER_SLAB
    col_group = (lax.broadcasted_iota(jnp.int32, (rows, cols), 1) // S5_STATE) % GROUPS_PER_SLAB
    return jnp.where(row_group == col_group, full.reshape(rows, cols), 0.0)


def _s5_kernel(uf_ref, ub_ref, cin_ref, cout_ref, wd_ref, a2_ref, yf_ref, yb_ref,
               win_scr, wout_scr, st_scr, carry_scr):
    @pl.when(pl.program_id(0) == 0)
    def _():
        st_scr[...] = jnp.zeros_like(st_scr)
        carry_scr[...] = jnp.zeros_like(carry_scr)
        for d in range(2):
            for j in range(N_SLAB):
                win_scr[d, j] = _s5_expand(cin_ref[d, j]).astype(BF16)
                wout_scr[d, j] = _s5_expand(cout_ref[d, j]).T.astype(BF16)

    def tile(m, piece):
        return [piece(slice(m * LANES, (m + 1) * LANES)),
                piece(slice(SLAB_STATE + m * LANES, SLAB_STATE + (m + 1) * LANES))]

    u_refs = (uf_ref, ub_ref)
    y_refs = (yf_ref, yb_ref)
    np_ = S5_TB // 2
    dot = lambda p, q: lax.dot_general(p, q, (((1,), (0,)), ((), ())), preferred_element_type=F32)
    for j in range(N_SLAB):
        lanes = slice(j * LANES, (j + 1) * LANES)
        for d in range(2):
            u = u_refs[d][:, lanes].astype(F32).reshape(np_, 2, BATCH, LANES)
            even, odd = u[:, 0].reshape(np_ * BATCH, LANES), u[:, 1].reshape(np_ * BATCH, LANES)
            lhs = jnp.concatenate([even, odd] if d == 0 else [odd, even], axis=1)
            out = None
            for m in range(S5_TILES):
                w_in = jnp.concatenate(tile(m, lambda c: win_scr[d, j, :, c]), axis=1)
                w_out = jnp.concatenate(tile(m, lambda c: wout_scr[d, j, c, :]), axis=0)
                z = dot(lhs, w_in)
                ar, ai = a2_ref[d, j, m, 0], a2_ref[d, j, m, 1]
                sr, si = st_scr[d, j, m, 0], st_scr[d, j, m, 1]
                states = [None] * np_
                for p in range(np_):
                    r = np_ - 1 - p if d == 1 else p
                    zr = z[r * BATCH:(r + 1) * BATCH]
                    sr, si = (ar * sr - ai * si + zr[:, 0:LANES],
                              ar * si + ai * sr + zr[:, LANES:])
                    states[r] = jnp.concatenate([sr, si], axis=1)
                st_scr[d, j, m, 0] = sr
                st_scr[d, j, m, 1] = si
                part = dot(jnp.concatenate(states, axis=0), w_out)
                out = part if out is None else out + part
            direct, ahead = out[:, 0:LANES], out[:, LANES:]
            carry = carry_scr[d, j]
            if d == 0:
                carry_scr[d, j] = ahead[(np_ - 1) * BATCH:]
                y_even = jnp.concatenate([carry, ahead[:(np_ - 1) * BATCH]], axis=0) + dot(even, wd_ref[d, j])
                y_odd = direct
            else:
                carry_scr[d, j] = ahead[:BATCH]
                y_odd = jnp.concatenate([ahead[BATCH:], carry], axis=0) + dot(odd, wd_ref[d, j])
                y_even = direct
            y = jnp.stack([y_even.reshape(np_, BATCH, LANES), y_odd.reshape(np_, BATCH, LANES)], axis=1)
            y_refs[d][:, lanes] = y.reshape(S5_TM, LANES).astype(y_refs[d].dtype)


def _s5_fwd_block(i):
    return jnp.where(i < S5_N_CTX, S5_N_LAT + i, i - S5_N_CTX)


def _s5_bwd_block(i):
    return S5_N_BLK - 1 - i


def _s5(l, u, c_in, c_out, w_d, a2):
    out = jax.ShapeDtypeStruct((ROWS_ALL, D_S5), BF16)
    return pl.pallas_call(
        _s5_kernel,
        grid=(S5_N_BLK,),
        in_specs=[
            pl.BlockSpec((S5_TM, D_S5), lambda i: (_s5_fwd_block(i), 0)),
            pl.BlockSpec((S5_TM, D_S5), lambda i: (_s5_bwd_block(i), 0)),
            _layer_spec(l, (2, N_SLAB, 2 * S5_GROUP, 2 * SLAB_STATE)),
            _layer_spec(l, (2, N_SLAB, 2 * S5_GROUP, 2 * SLAB_STATE)),
            _layer_spec(l, (2, N_SLAB, LANES, LANES)),
            _layer_spec(l, (2, N_SLAB, S5_TILES, 2, BATCH, LANES)),
        ],
        out_specs=[
            pl.BlockSpec((S5_TM, D_S5), lambda i: (_s5_fwd_block(i), 0)),
            pl.BlockSpec((S5_TM, D_S5), lambda i: (_s5_bwd_block(i), 0)),
        ],
        out_shape=[out, out],
        scratch_shapes=[pltpu.VMEM((2, N_SLAB, 2 * LANES, 2 * SLAB_STATE), BF16),
                        pltpu.VMEM((2, N_SLAB, 2 * SLAB_STATE, 2 * LANES), BF16),
                        pltpu.VMEM((2, N_SLAB, S5_TILES, 2, BATCH, LANES), F32),
                        pltpu.VMEM((2, N_SLAB, BATCH, LANES), F32)],
        compiler_params=_params(("arbitrary",)),
        name="s5_scan",
    )(u, u, c_in, c_out, w_d, a2)


def _lru_conv(xe, n, cw_ref, cb_ref):
    out = xe[0:n] * cw_ref[0:1, :][None]
    for k in range(1, CONV_W):
        out = out + xe[k:k + n] * cw_ref[k:k + 1, :][None]
    return out + cb_ref[...][None]


def _lru_coeffs(xc, d, wg_ref, bg_ref, lam_ref, a_scr, b_scr):
    n = xc.shape[0]
    x2 = xc.reshape(n * BATCH, D_LRU)
    for hb in range(D_LRU // LRU_HALF):
        sl = slice(hb * LRU_HALF, (hb + 1) * LRU_HALF)
        xh = x2[:, sl]
        g = jnp.dot(xh.astype(BF16), wg_ref[d, hb], preferred_element_type=F32)
        tr = jnp.tanh(g[:, 0:LRU_HALF] + bg_ref[d, 0:1, sl])
        ti = jnp.tanh(g[:, LRU_HALF:] + bg_ref[d, 1:2, sl])
        lam = lam_ref[d:d + 1, sl]
        softplus = jnp.maximum(-lam, 0.0) + jnp.log1p(jnp.exp(-jnp.abs(lam)))
        k = (-0.5 * LRU_C * math.log2(math.e)) * softplus
        a = jnp.exp2(k * tr + k)
        rest = 1.0 - a * a
        mult = jnp.where(rest > 0.0, rest * lax.rsqrt(rest), 0.0)
        b = mult * (ti * xh + xh)
        a_scr[d, :, :, sl] = a.reshape(n, BATCH, LRU_HALF)
        b_scr[d, :, :, sl] = b.reshape(n, BATCH, LRU_HALF)


def _lru_scan(n, hf0, hb0, a_scr, b_scr, write_f, write_b):
    def body(r, carry):
        hf, hb = carry
        rb = n - 1 - r
        hf = a_scr[0, r] * hf + b_scr[0, r]
        hb = a_scr[1, rb] * hb + b_scr[1, rb]
        write_f(r, hf)
        write_b(rb, hb)
        return hf, hb
    return lax.fori_loop(0, n, body, (hf0, hb0), unroll=4)


def _lru_kernel(n_runs, cols, n_aliased, xf_ref, xfp_ref, xfn_ref, xb_ref, xbp_ref, xbn_ref,
                cw_ref, cb_ref, wg_ref, bg_ref, lam_ref, st0_ref, *refs):
    hf_ref, hb_ref, st_ref, a_scr, b_scr = refs[n_aliased:]
    i = pl.program_id(0)
    n = LRU_RUN * max(cols, 1)

    @pl.when(i == 0)
    def _():
        st_ref[...] = st0_ref[...]

    def run(x_ref, prev_ref, next_ref, w):
        has_prev = (w > 0).astype(F32)
        has_next = (w < n_runs - 1).astype(F32)
        xs = [x_ref[...]] if cols == 0 else [x_ref[:, c] for c in range(cols)]
        xe = jnp.concatenate([prev_ref[...] * has_prev] + xs + [next_ref[...] * has_next], axis=0)
        return _lru_conv(xe, n, cw_ref, cb_ref)

    _lru_coeffs(run(xf_ref, xfp_ref, xfn_ref, i), 0, wg_ref, bg_ref, lam_ref, a_scr, b_scr)
    _lru_coeffs(run(xb_ref, xbp_ref, xbn_ref, n_runs - 1 - i), 1, wg_ref, bg_ref, lam_ref,
                a_scr, b_scr)

    def write(ref, t, h):
        if cols == 0:
            ref[t] = h
        else:
            ref[lax.rem(t, GRID_H), lax.div(t, GRID_H)] = h

    write_f = functools.partial(write, hf_ref)
    write_b = functools.partial(write, hb_ref)

    hf, hb = _lru_scan(n, st_ref[0], st_ref[1], a_scr, b_scr, write_f, write_b)
    st_ref[0] = hf
    st_ref[1] = hb


def _lru_call(name, l, n_runs, cols, run_spec, prev_spec, next_spec, x4, weights, st0, hf_in, hb_in):
    full = jax.ShapeDtypeStruct((N_BLK, TB, BATCH, D_LRU), F32)
    st_spec = pl.BlockSpec((2, BATCH, D_LRU), lambda i: (0, 0, 0))
    wspecs = [
        _layer_spec(l, (CONV_W, D_LRU)),
        _layer_spec(l, (1, D_LRU)),
        _layer_spec(l, (2, 2, LRU_HALF, 2 * LRU_HALF)),
        _layer_spec(l, (2, 2, D_LRU)),
        _layer_spec(l, (2, D_LRU)),
    ]
    fw = lambda i: i
    bw = lambda i: n_runs - 1 - i
    aliased = [] if hf_in is None else [hf_in, hb_in]
    n_in = 6 + len(wspecs) + 1
    return pl.pallas_call(
        functools.partial(_lru_kernel, n_runs, cols, len(aliased)),
        grid=(n_runs,),
        in_specs=[run_spec(fw), prev_spec(fw), next_spec(fw),
                  run_spec(bw), prev_spec(bw), next_spec(bw)] + wspecs
        + [st_spec] + [pl.BlockSpec(memory_space=pl.ANY)] * len(aliased),
        out_specs=[run_spec(fw), run_spec(bw), st_spec],
        out_shape=[full, full, jax.ShapeDtypeStruct((2, BATCH, D_LRU), F32)],
        scratch_shapes=[pltpu.VMEM((2, LRU_RUN * max(cols, 1), BATCH, D_LRU), F32)] * 2,
        input_output_aliases={n_in + k: k for k in range(len(aliased))},
        compiler_params=_params(("arbitrary",)),
        name=name,
    )(x4, x4, x4, x4, x4, x4, *weights, st0, *aliased)


def _lru(l, xr, weights):
    x4 = xr.reshape(N_BLK, TB, BATCH, D_LRU)
    st0 = jnp.zeros((2, BATCH, D_LRU), F32)

    c0 = N_LAT_BLK
    clip = lambda k, n: jnp.clip(k, 0, n - 1)
    ctx_run = lambda f: pl.BlockSpec((None, TB, BATCH, D_LRU), lambda i: (c0 + f(i), 0, 0, 0))
    ctx_prev = lambda f: pl.BlockSpec(
        (None, CONV_LEFT, BATCH, D_LRU),
        lambda i: (c0 + clip(f(i) - 1, N_CTX_BLK), TB // CONV_LEFT - 1, 0, 0))
    ctx_next = lambda f: pl.BlockSpec(
        (None, 1, BATCH, D_LRU), lambda i: (c0 + clip(f(i) + 1, N_CTX_BLK), 0, 0, 0))
    hf, hb, st = _lru_call("lru_ctx", l, N_CTX_BLK, 0, ctx_run, ctx_prev, ctx_next,
                           x4, weights, st0, None, None)

    lat_run = lambda f: pl.BlockSpec((GRID_H, LRU_COLS, BATCH, D_LRU), lambda i: (0, f(i), 0, 0))
    lat_prev = lambda f: pl.BlockSpec(
        (CONV_LEFT, None, BATCH, D_LRU),
        lambda i: (GRID_H // CONV_LEFT - 1, clip(f(i) * LRU_COLS - 1, GRID_W), 0, 0))
    lat_next = lambda f: pl.BlockSpec(
        (1, None, BATCH, D_LRU), lambda i: (0, clip((f(i) + 1) * LRU_COLS, GRID_W), 0, 0))
    hf, hb, _ = _lru_call("lru_lat", l, GRID_W // LRU_COLS, LRU_COLS, lat_run, lat_prev, lat_next,
                          x4, weights, st, hf, hb)
    return hf.reshape(ROWS_ALL, D_LRU), hb.reshape(ROWS_ALL, D_LRU)


def _tail_kernel(first, has_next, *refs):
    if first:
        x_in = _input_rows(refs[0], refs[1])
        load_x = lambda rows: x_in[rows]
        refs = refs[2:]
    else:
        x_ref = refs[0]
        load_x = lambda rows: x_ref[rows, :]
        refs = refs[1:]
    (u_ref, yf_ref, yb_ref, gr_ref, hf_ref, hb_ref, mod_ref,
     d_ref, wglu_ref, bglu_ref, wout_ref, gains_ref, wfin_ref, wfout_ref) = refs[:14]
    refs = refs[14:]
    mod = mod_ref[0]
    chunk = lambda k: mod[:, k * D_MODEL:(k + 1) * D_MODEL]
    gate1, shift2, scale2, gate2 = chunk(2), chunk(3), chunk(4), chunk(5)
    groups = [slice(s * TAIL_ROWS, (s + 1) * TAIL_ROWS) for s in range(TAIL_SPLIT)]

    def mixer_out(rows):
        f32 = lambda ref: ref[rows, :].astype(F32)
        ys = _gelu(d_ref[...] * f32(u_ref) + f32(yf_ref) + f32(yb_ref))
        z = jnp.dot(ys.astype(BF16), wglu_ref[...], preferred_element_type=F32) + bglu_ref[...]
        y_s5 = ys * _sigmoid(z)
        y_lru = (f32(hf_ref) + f32(hb_ref)) * _gelu(f32(gr_ref))
        out = jnp.dot(y_s5.astype(BF16), wout_ref[0:D_S5, :], preferred_element_type=F32)
        out = out + jnp.dot(y_lru.astype(BF16), wout_ref[D_S5:, :], preferred_element_type=F32)
        return load_x(rows) + _per_batch(_rms(out, gains_ref[1:2, :]), lambda t: t * gate1[None])

    def ffn(x):
        h = _per_batch(_rms(x, gains_ref[2:3, :]), lambda t: t * (1.0 + scale2)[None] + shift2[None])
        h = h.astype(BF16)
        f = jnp.zeros((TAIL_ROWS, D_MODEL), F32)
        for start, size in FF_CHUNKS:
            gt = jnp.dot(h, wfin_ref[:, start:start + size], preferred_element_type=F32)
            up = jnp.dot(h, wfin_ref[:, D_FF + start:D_FF + start + size], preferred_element_type=F32)
            act = (gt * _sigmoid(gt) * up).astype(BF16)
            f = f + jnp.dot(act, wfout_ref[start:start + size, :], preferred_element_type=F32)
        return x + _per_batch(_rms(f, gains_ref[3:4, :]), lambda t: t * gate2[None])

    xs = [mixer_out(rows) for rows in groups]
    xs = [ffn(x) for x in xs]
    for s, (rows, x) in enumerate(zip(groups, xs)):
        if has_next:
            modn_ref, gn_ref, win_ref, o_ref, un_ref, xrn_ref, grn_ref = refs
            o_ref[rows, :] = x
            _premix_math(x, modn_ref[0], gn_ref[0:1, :], win_ref,
                         un_ref.at[rows, :], xrn_ref.at[rows, :], grn_ref.at[rows, :])
        else:
            (o_ref,) = refs
            steps = slice(s * TAIL_ROWS // BATCH, (s + 1) * TAIL_ROWS // BATCH)
            o_ref[:, steps, :] = jnp.swapaxes(x.reshape(TAIL_ROWS // BATCH, BATCH, D_MODEL), 0, 1)


def _tail(l, x_src, u, yf, yb, gr, hf, hb, mod, d, wglu, bglu, wout, gains, wfin, wfout, w_in):
    first = len(x_src) == 2
    has_next = l < DEPTH - 1
    n_blk = N_BLK if has_next else N_LAT_BLK
    rows = n_blk * TM
    row = lambda i: (i, 0)
    once = pl.Buffered(1)
    wide = pl.BlockSpec((TM, D_MODEL), row)
    half = pl.BlockSpec((TM, D_S5), row)
    in_specs = ([_X_SPEC, _CTX_SPEC] if first else [wide]) + [half] * 6 + [
        _mod_spec(l),
        _layer_spec(l, (1, D_S5)),
        _layer_spec(l, (D_S5, D_S5), pipeline_mode=once),
        _layer_spec(l, (1, D_S5)),
        _layer_spec(l, (D_MODEL, D_MODEL), pipeline_mode=once),
        _layer_spec(l, (4, D_MODEL)),
        _layer_spec(l, (D_MODEL, 2 * D_FF), pipeline_mode=once),
        _layer_spec(l, (D_FF, D_MODEL), pipeline_mode=once),
    ]
    args = [*x_src, u, yf, yb, gr, hf, hb, mod, d, wglu, bglu, wout, gains, wfin, wfout]
    if has_next:
        in_specs += [_mod_spec(l + 1), _layer_spec(l + 1, (4, D_MODEL)),
                     _layer_spec(l + 1, (D_MODEL, D_IN), pipeline_mode=once)]
        args += [mod, gains, w_in]
        out_specs = [wide] + [half] * 3
        out_shape = [jax.ShapeDtypeStruct((rows, D_MODEL), F32)]
        out_shape += [jax.ShapeDtypeStruct((rows, D_S5), dt) for dt in (BF16, F32, BF16)]
    else:
        out_specs = [pl.BlockSpec((BATCH, TB, D_MODEL), lambda i: (0, i, 0))]
        out_shape = [jax.ShapeDtypeStruct((BATCH, SEQ, D_MODEL), F32)]
    return pl.pallas_call(
        functools.partial(_tail_kernel, first, has_next),
        grid=(n_blk,),
        in_specs=in_specs,
        out_specs=out_specs,
        out_shape=out_shape,
        compiler_params=_params(("parallel",)),
        name="tail",
    )(*args)


def _s5_matrices(a_re, a_im, log_dt, b_re, b_im, c_re, c_im):
    dt = jnp.exp(log_dt)[..., None]
    mag = jnp.exp(a_re * dt)
    ab_re = mag * jnp.cos(a_im * dt)
    ab_im = mag * jnp.sin(a_im * dt)
    den = a_re * a_re + a_im * a_im
    nr = ab_re - 1.0
    f_re = (nr * a_re + ab_im * a_im) / den
    f_im = (ab_im * a_re - nr * a_im) / den
    bb_re = f_re[..., None] * b_re - f_im[..., None] * b_im
    bb_im = f_re[..., None] * b_im + f_im[..., None] * b_re

    def compact(m):
        m = m.reshape(DEPTH, 2, N_SLAB, GROUPS_PER_SLAB, 2, 2, S5_GROUP, S5_STATE)
        return m.transpose(0, 1, 2, 4, 6, 5, 3, 7).reshape(DEPTH, 2, N_SLAB, 2 * S5_GROUP, 2 * SLAB_STATE)

    ar, ai = ab_re[..., None, :], ab_im[..., None, :]
    br, bi = jnp.swapaxes(bb_re, -1, -2), jnp.swapaxes(bb_im, -1, -2)
    first = jnp.stack([br * ar - bi * ai, br * ai + bi * ar], axis=3)
    c_in = compact(jnp.stack([first, jnp.stack([br, bi], axis=3)], axis=3))
    now = jnp.stack([c_re, -c_im], axis=3)
    ahead = jnp.stack([c_re * ar - c_im * ai, -(c_re * ai + c_im * ar)], axis=3)
    c_out = compact(jnp.stack([now, ahead], axis=3))

    hi = lax.Precision.HIGHEST
    d_group = (jnp.einsum('ldgnp,ldgqn->ldgpq', bb_re, c_re, precision=hi)
               - jnp.einsum('ldgnp,ldgqn->ldgpq', bb_im, c_im, precision=hi))
    d_group = d_group.reshape(DEPTH, 2, N_SLAB, GROUPS_PER_SLAB, S5_GROUP, S5_GROUP)
    w_d = jnp.einsum('ldjgpq,gh->ldjgphq', d_group, jnp.eye(GROUPS_PER_SLAB, dtype=F32))
    w_d = w_d.reshape(DEPTH, 2, N_SLAB, LANES, LANES).astype(BF16)

    tiles = lambda m: m.reshape(DEPTH, 2, N_SLAB, S5_TILES, LANES)
    a2 = jnp.stack([tiles(ab_re * ab_re - ab_im * ab_im), tiles(2.0 * ab_re * ab_im)], axis=4)
    a2 = jnp.broadcast_to(a2[..., None, :], (DEPTH, 2, N_SLAB, S5_TILES, 2, BATCH, LANES))
    return c_in, c_out, w_d, a2


def _lru_gate_matrix(w_rg, w_ig):
    heads = LRU_HALF // LRU_HEAD_DIM
    halves = D_LRU // LRU_HALF
    eye = jnp.eye(heads, dtype=F32)

    def bd(w):
        w = w.reshape(DEPTH, 2, halves, heads, LRU_HEAD_DIM, LRU_HEAD_DIM)
        return jnp.einsum('ldbhij,hk->ldbhikj', w, eye).reshape(DEPTH, 2, halves, LRU_HALF, LRU_HALF)

    return jnp.concatenate([bd(w_rg), bd(w_ig)], axis=-1).astype(BF16)


def kernel(x, c, ctx, c_ctx, w_ada, b_ada, norm_gains, w_in, s5_a_re, s5_a_im, s5_log_dt, s5_b_re, s5_b_im, s5_c_re, s5_c_im, s5_d, s5_w_glu, s5_b_glu, lru_conv_w, lru_conv_b, lru_w_rg, lru_b_rg, lru_w_ig, lru_b_ig, lru_lambda, w_out, w_ffn_in, w_ffn_out):
    assert x.shape == (BATCH, SEQ, D_MODEL) and ctx.shape == (BATCH, CTX_LEN, D_MODEL)
    cond = jnp.concatenate([c, c_ctx[None], jnp.zeros((2 * BATCH - BATCH - 1, D_MODEL), F32)], axis=0)
    m = _adaln(cond, w_ada, b_ada)
    mod = jnp.stack(
        [m[:, :BATCH], jnp.broadcast_to(m[:, BATCH:BATCH + 1], (DEPTH, BATCH, 6 * D_MODEL))], axis=1)

    w_in, w_glu, w_out, w_ffn_in, w_ffn_out = (
        w.astype(BF16) for w in (w_in, s5_w_glu, w_out, w_ffn_in, w_ffn_out))
    s5_w = _s5_matrices(s5_a_re, s5_a_im, s5_log_dt, s5_b_re, s5_b_im, s5_c_re, s5_c_im)
    lru_w = (0.5 * lru_conv_w, 0.5 * lru_conv_b[:, None], _lru_gate_matrix(lru_w_rg, lru_w_ig),
             0.5 * jnp.stack([lru_b_rg, lru_b_ig], axis=2), lru_lambda)

    u, xr, gr = _premix(0, x, ctx, mod, norm_gains, w_in)
    x_src = (x, ctx)
    for l in range(DEPTH):
        yf, yb = _s5(l, u, *s5_w)
        hf, hb = _lru(l, xr, lru_w)
        outs = _tail(l, x_src, u, yf, yb, gr, hf, hb, mod, s5_d[:, None], w_glu, s5_b_glu[:, None],
                     w_out, norm_gains, w_ffn_in, w_ffn_out, w_in)
        if l < DEPTH - 1:
            x_all, u, xr, gr = outs
            x_src = (x_all,)
    return outs[0]
```
